```python
import math
import jax, jax.numpy as jnp
from jax import lax
import numpy as np

D_MODEL = 1024
BATCH = 8
SEQ = 2048
DEPTH = 1
DEC_BATCH = 128
DEC_SEQ = 1
PAST_LEN = 8192
PAGE_SIZE = 128

D_RNN = 1280
RNN_BLOCKS = 16
RNN_BW = D_RNN // RNN_BLOCKS
CONV_W = 4
LRU_C = 8.0
N_HEADS = 16
N_KV = 4
HEAD_DIM = 64
GROUP = N_HEADS // N_KV
WINDOW = 128
ROPE_THETA = 10000.0
Q_W = N_HEADS * HEAD_DIM
KV_W = N_KV * HEAD_DIM
PEER_HEADS = 8
N_KEYS = 128
N_EXPERTS = N_KEYS * N_KEYS
PEER_DK = 256
PEER_DHALF = PEER_DK // 2
PEER_TOPK = 16
PEER_HK = PEER_HEADS * PEER_TOPK
PEER_CHUNK = 64
EPS = 1e-6
NEG = -1e30
IN_SPLITS = [D_RNN, 2 * D_RNN, 2 * D_RNN + Q_W, 2 * D_RNN + Q_W + KV_W,
             2 * D_RNN + Q_W + 2 * KV_W, 2 * D_RNN + Q_W + 2 * KV_W + D_MODEL]
D_IN = 2 * D_RNN + Q_W + 2 * KV_W + 2 * D_MODEL

kernel_name = "hybrid_rglru_swa_peer_decode_step"


def rmsnorm(x, g):
    xf = x.astype(jnp.float32)
    y = xf * lax.rsqrt(jnp.mean(xf * xf, axis=-1, keepdims=True) + EPS) * g.astype(jnp.float32)
    return y.astype(x.dtype)


def modulate(xn, shift, scale):
    return xn * (1.0 + scale[:, None, :]) + shift[:, None, :]


def rope(x, pos):
    half = HEAD_DIM // 2
    inv = ROPE_THETA ** (-jnp.arange(half, dtype=jnp.float32) / half)
    ang = pos.astype(jnp.float32)[:, None] * inv[None, :]
    cos = jnp.cos(ang)[:, None, :]
    sin = jnp.sin(ang)[:, None, :]
    xf = x.astype(jnp.float32)
    x1, x2 = xf[..., :half], xf[..., half:]
    return jnp.concatenate([x1 * cos - x2 * sin, x2 * cos + x1 * sin], axis=-1).astype(x.dtype)


def causal_conv(xb, buf, w, b):
    xp = jnp.concatenate([buf.astype(xb.dtype), xb], axis=1)
    y = lax.conv_general_dilated(xp, w.astype(xp.dtype)[:, None, :], window_strides=(1,),
                                 padding='VALID', dimension_numbers=('NWC', 'WIO', 'NWC'),
                                 feature_group_count=D_RNN)
    return y + b.astype(y.dtype), xp[:, -(CONV_W - 1):]


def rglru(xc, h0, w_a, b_a, w_i, b_i, lam):
    B, T, _ = xc.shape
    xf = xc.astype(jnp.float32)
    xblk = xf.reshape(B, T, RNN_BLOCKS, RNN_BW)
    r = jax.nn.sigmoid(jnp.einsum('btnj,njk->btnk', xblk, w_a.astype(jnp.float32)).reshape(B, T, D_RNN)
                       + b_a.astype(jnp.float32))
    i = jax.nn.sigmoid(jnp.einsum('btnj,njk->btnk', xblk, w_i.astype(jnp.float32)).reshape(B, T, D_RNN)
                       + b_i.astype(jnp.float32))
    log_a = LRU_C * r * jax.nn.log_sigmoid(lam.astype(jnp.float32))
    a = jnp.exp(log_a)
    bt = jnp.sqrt(-jnp.expm1(2.0 * log_a)) * (i * xf)
    bt = bt.at[:, 0].add(a[:, 0] * h0.astype(jnp.float32))

    def comb(lhs, rhs):
        a1, b1 = lhs
        a2, b2 = rhs
        return a1 * a2, a2 * b1 + b2

    _, h = lax.associative_scan(comb, (a, bt), axis=1)
    return h, h[:, -1]


def attn_core(q, k, v, mask, sinks):
    s = jnp.einsum('...qhgd,...khd->...hgqk', q, k).astype(jnp.float32) * (HEAD_DIM ** -0.5)
    s = jnp.where(mask, s, NEG)
    sk = sinks.astype(jnp.float32).reshape(N_KV, GROUP)[:, :, None, None]
    m = jnp.maximum(jnp.max(s, axis=-1, keepdims=True), sk)
    e = jnp.exp(s - m)
    p = e / (jnp.sum(e, axis=-1, keepdims=True) + jnp.exp(sk - m))
    return jnp.einsum('...hgqk,...khd->...qhgd', p.astype(v.dtype), v)


def window_mask(qpos, kpos):
    d = qpos[..., :, None] - kpos[..., None, :]
    return (d >= 0) & (d <= WINDOW) & (kpos[..., None, :] >= 0)


def attn_prompt(q, k, v, sinks):
    B, S = q.shape[0], q.shape[1]
    nb = S // WINDOW
    qb = q.reshape(B, nb, WINDOW, N_KV, GROUP, HEAD_DIM)
    kb = k.reshape(B, nb, WINDOW, N_KV, HEAD_DIM)
    vb = v.reshape(B, nb, WINDOW, N_KV, HEAD_DIM)
    kk = jnp.concatenate([jnp.concatenate([jnp.zeros_like(kb[:, :1]), kb[:, :-1]], axis=1), kb], axis=2)
    vv = jnp.concatenate([jnp.concatenate([jnp.zeros_like(vb[:, :1]), vb[:, :-1]], axis=1), vb], axis=2)
    blk = jnp.arange(nb)[:, None] * WINDOW
    qpos = blk + jnp.arange(WINDOW)[None, :]
    kpos = blk - WINDOW + jnp.arange(2 * WINDOW)[None, :]
    mask = window_mask(qpos, kpos)[:, None, None]
    o = attn_core(qb, kk, vv, mask, sinks)
    nbuf = min(WINDOW, S)
    return o.reshape(B, S, Q_W), k[:, -nbuf:], v[:, -nbuf:]


def attn_sample(q, k, v, buf_k, buf_v, sinks, past_len):
    B, T = q.shape[0], q.shape[1]
    nbuf = buf_k.shape[1]
    kk = jnp.concatenate([buf_k.astype(k.dtype), k], axis=1)
    vv = jnp.concatenate([buf_v.astype(v.dtype), v], axis=1)
    kpos = jnp.concatenate([past_len - nbuf + jnp.arange(nbuf), past_len + jnp.arange(T)])
    qpos = past_len + jnp.arange(T)
    mask = window_mask(qpos, kpos)[None, None]
    o = attn_core(q.reshape(B, T, N_KV, GROUP, HEAD_DIM), kk, vv, mask, sinks)
    return o.reshape(B, T, Q_W), kk[:, -nbuf:], vv[:, -nbuf:]


def peer(h, w_pq, sub_keys, u_tab, v_tab):
    B, T, D = h.shape
    n = B * T
    hf = h.reshape(n, D)
    q = (hf @ w_pq).reshape(n, PEER_HEADS, 2, PEER_DHALF)
    s = jnp.einsum('nhsd,hskd->nhsk', q.astype(jnp.float32), sub_keys.astype(jnp.float32))
    sv, si = lax.top_k(s, PEER_TOPK)
    cand = (sv[:, :, 0, :, None] + sv[:, :, 1, None, :]).reshape(n, PEER_HEADS, PEER_TOPK * PEER_TOPK)
    cidx = (si[:, :, 0, :, None] * N_KEYS + si[:, :, 1, None, :]).reshape(n, PEER_HEADS, PEER_TOPK * PEER_TOPK)
    fv, fp = lax.top_k(cand, PEER_TOPK)
    eidx = jnp.take_along_axis(cidx, fp, axis=-1).reshape(n, PEER_HK)
    g = jax.nn.softmax(fv, axis=-1).reshape(n, PEER_HK)
    pad = (-n) % PEER_CHUNK
    hp = jnp.pad(hf, ((0, pad), (0, 0)))
    ep = jnp.pad(eidx, ((0, pad), (0, 0)))
    gp = jnp.pad(g, ((0, pad), (0, 0)))
    nc = (n + pad) // PEER_CHUNK

    def chunk(args):
        hc, ec, gc = args
        u = jnp.take(u_tab, ec, axis=0)
        act = jax.nn.gelu(jnp.einsum('cd,ced->ce', hc, u).astype(jnp.float32), approximate=False)
        wgt = (gc * act).astype(hc.dtype)
        vr = jnp.take(v_tab, ec, axis=0)
        return jnp.einsum('ce,ced->cd', wgt, vr)

    out = lax.map(chunk, (hp.reshape(nc, PEER_CHUNK, D), ep.reshape(nc, PEER_CHUNK, PEER_HK),
                          gp.reshape(nc, PEER_CHUNK, PEER_HK)))
    return out.reshape(nc * PEER_CHUNK, D)[:n].reshape(B, T, D)


def layer(x, c, pos, conv_buf, h0, attn_fn, lp):
    mod = jax.nn.silu(c) @ lp['w_mod'] + lp['b_mod']
    sh1, sc1, gt1, sh2, sc2, gt2 = jnp.split(mod, 6, axis=-1)
    B, T, _ = x.shape
    hn = modulate(rmsnorm(x, lp['g_norm1']), sh1, sc1)
    z = hn @ lp['w_in']
    xr, gr, q, k, v, gate_r, gate_a = jnp.split(z, IN_SPLITS, axis=-1)
    xc, conv_new = causal_conv(xr, conv_buf, lp['conv_w'], lp['conv_b'])
    hseq, h_last = rglru(xc, h0, lp['w_a'], lp['b_a'], lp['w_i'], lp['b_i'], lp['lam'])
    y_r = (hseq.astype(x.dtype) * jax.nn.gelu(gr, approximate=False)) @ lp['w_rnn_out']
    q = rope(q.reshape(B, T, N_HEADS, HEAD_DIM), pos)
    k = rope(k.reshape(B, T, N_KV, HEAD_DIM), pos)
    v = v.reshape(B, T, N_KV, HEAD_DIM)
    o, k_st, v_st = attn_fn(q, k, v, lp['sinks'])
    y_a = o @ lp['w_attn_out']
    merged = jax.nn.sigmoid(gate_r) * y_r + jax.nn.sigmoid(gate_a) * y_a
    x1 = x + gt1[:, None, :] * (merged @ lp['w_out'])
    h2 = modulate(rmsnorm(x1, lp['g_norm2']), sh2, sc2)
    x2 = x1 + gt2[:, None, :] * peer(h2, lp['w_pq'], lp['sub_keys'], lp['u_tab'], lp['v_tab'])
    return x2, conv_new, h_last, k_st, v_st


def setup_inputs(seed: int = 0) -> dict:
    key = jax.random.key(seed)
    ks = jax.random.split(key, 40)
    f32 = jnp.float32
    W_BUF = min(WINDOW, PAST_LEN)

    def nrm(k, shape, scale):
        return jax.random.normal(k, shape, f32) * scale

    a0 = jax.random.uniform(ks[20], (DEPTH, D_RNN), f32, 0.9, 0.999)
    ab = a0 ** (1.0 / LRU_C)
    lam = jnp.log(ab) - jnp.log1p(-ab)
    return {
        "x_prompt": nrm(ks[0], (BATCH, SEQ, D_MODEL), 1.0),
        "x_sample": nrm(ks[1], (DEC_BATCH, DEC_SEQ, D_MODEL), 1.0),
        "state_conv": nrm(ks[2], (DEPTH, DEC_BATCH, CONV_W - 1, D_RNN), 0.5),
        "state_rnn": nrm(ks[3], (DEPTH, DEC_BATCH, D_RNN), 0.5),
        "cache_win_k": nrm(ks[4], (DEPTH, DEC_BATCH, W_BUF, N_KV, HEAD_DIM), 1.0),
        "cache_win_v": nrm(ks[5], (DEPTH, DEC_BATCH, W_BUF, N_KV, HEAD_DIM), 1.0),
        "c_prompt": nrm(ks[6], (BATCH, D_MODEL), 1.0),
        "c_sample": nrm(ks[7], (DEC_BATCH, D_MODEL), 1.0),
        "g_norm1": 1.0 + nrm(ks[8], (DEPTH, D_MODEL), 0.02),
        "g_norm2": 1.0 + nrm(ks[9], (DEPTH, D_MODEL), 0.02),
        "g_final": 1.0 + nrm(ks[10], (D_MODEL,), 0.02),
        "w_mod": nrm(ks[11], (DEPTH, D_MODEL, 6 * D_MODEL), D_MODEL ** -0.5),
        "b_mod": nrm(ks[12], (DEPTH, 6 * D_MODEL), 0.02),
        "w_in": nrm(ks[13], (DEPTH, D_MODEL, D_IN), D_MODEL ** -0.5),
        "conv_w": nrm(ks[14], (DEPTH, CONV_W, D_RNN), CONV_W ** -0.5),
        "conv_b": nrm(ks[15], (DEPTH, D_RNN), 0.02),
        "w_a": nrm(ks[16], (DEPTH, RNN_BLOCKS, RNN_BW, RNN_BW), RNN_BW ** -0.5),
        "b_a": nrm(ks[17], (DEPTH, D_RNN), 0.02),
        "w_i": nrm(ks[18], (DEPTH, RNN_BLOCKS, RNN_BW, RNN_BW), RNN_BW ** -0.5),
        "b_i": nrm(ks[19], (DEPTH, D_RNN), 0.02),
        "lam": lam,
        "w_rnn_out": nrm(ks[21], (DEPTH, D_RNN, D_MODEL), D_RNN ** -0.5),
        "sinks": nrm(ks[22], (DEPTH, N_HEADS), 0.5),
        "w_attn_out": nrm(ks[23], (DEPTH, Q_W, D_MODEL), Q_W ** -0.5),
        "w_out": nrm(ks[24], (DEPTH, D_MODEL, D_MODEL), D_MODEL ** -0.5),
        "w_pq": nrm(ks[25], (DEPTH, D_MODEL, PEER_HEADS * PEER_DK), D_MODEL ** -0.5),
        "sub_keys": nrm(ks[26], (DEPTH, PEER_HEADS, 2, N_KEYS, PEER_DHALF), PEER_DHALF ** -0.5),
        "u_tab": nrm(ks[27], (DEPTH, N_EXPERTS, D_MODEL), D_MODEL ** -0.5),
        "v_tab": nrm(ks[28], (DEPTH, N_EXPERTS, D_MODEL), PEER_HEADS ** -0.5),
    }


def reference(x_prompt, x_sample, state_conv, state_rnn, cache_win_k, cache_win_v, c_prompt, c_sample,
              g_norm1, g_norm2, g_final, w_mod, b_mod, w_in, conv_w, conv_b, w_a, b_a, w_i, b_i, lam,
              w_rnn_out, sinks, w_attn_out, w_out, w_pq, sub_keys, u_tab, v_tab):
    xp, xs = x_prompt, x_sample
    pos_p = jnp.arange(xp.shape[1])
    pos_s = PAST_LEN + jnp.arange(xs.shape[1])
    cp_l, rp_l, kp_l, vp_l = [], [], [], []
    cs_l, rs_l, ksl, vsl = [], [], [], []
    for l in range(DEPTH):
        lp = dict(g_norm1=g_norm1[l], g_norm2=g_norm2[l], w_mod=w_mod[l], b_mod=b_mod[l], w_in=w_in[l],
                  conv_w=conv_w[l], conv_b=conv_b[l], w_a=w_a[l], b_a=b_a[l], w_i=w_i[l], b_i=b_i[l],
                  lam=lam[l], w_rnn_out=w_rnn_out[l], sinks=sinks[l], w_attn_out=w_attn_out[l],
                  w_out=w_out[l], w_pq=w_pq[l], sub_keys=sub_keys[l], u_tab=u_tab[l], v_tab=v_tab[l])
        conv0 = jnp.zeros((xp.shape[0], CONV_W - 1, D_RNN), xp.dtype)
        h00 = jnp.zeros((xp.shape[0], D_RNN), jnp.float32)
        xp, cpn, rpn, kpn, vpn = layer(xp, c_prompt, pos_p, conv0, h00, attn_prompt, lp)
        bk, bv = cache_win_k[l], cache_win_v[l]
        s_attn = lambda q, k, v, sk, bk=bk, bv=bv: attn_sample(q, k, v, bk, bv, sk, PAST_LEN)
        xs, csn, rsn, ksn, vsn = layer(xs, c_sample, pos_s, state_conv[l], state_rnn[l], s_attn, lp)
        cp_l.append(cpn); rp_l.append(rpn); kp_l.append(kpn); vp_l.append(vpn)
        cs_l.append(csn); rs_l.append(rsn); ksl.append(ksn); vsl.append(vsn)
    y_prompt = rmsnorm(xp, g_final)
    y_sample = rmsnorm(xs, g_final)
    return (y_prompt, y_sample,
            jnp.stack(cp_l), jnp.stack(rp_l), jnp.stack(kp_l), jnp.stack(vp_l),
            jnp.stack(cs_l), jnp.stack(rs_l), jnp.stack(ksl), jnp.stack(vsl))
```

```python
import functools

import jax
import jax.numpy as jnp
from jax import lax
from jax.experimental import pallas as pl
from jax.experimental.pallas import tpu as pltpu

F32 = jnp.float32
BF16 = jnp.bfloat16

D_MODEL = 1024
PAST_LEN = 8192
D_RNN = 1280
RNN_BLOCKS = 16
RNN_BW = D_RNN // RNN_BLOCKS
CONV_W = 4
LRU_C = 8.0
N_HEADS = 16
N_KV = 4
HEAD_DIM = 64
GROUP = N_HEADS // N_KV
WINDOW = 128
ROPE_THETA = 10000.0
Q_W = N_HEADS * HEAD_DIM
KV_W = N_KV * HEAD_DIM
PEER_HEADS = 8
N_KEYS = 128
N_EXPERTS = N_KEYS * N_KEYS
PEER_DK = 256
PEER_DHALF = PEER_DK // 2
PEER_TOPK = 16
EPS = 1e-6
NEG = -1e30
GATE_W = 2 * D_MODEL

LANES = 128
SUBLANES = 8
MIB = 1024 * 1024

TOKEN_TILE = 256
PEER_TOKEN_TILE = 512
PEER_LANE_CHUNK = 256
PEER_EXPERT_BLOCK = 1024
PEER_A_PER_BLOCK = PEER_EXPERT_BLOCK // N_KEYS


def _params(semantics, vmem_mib):
    return pltpu.CompilerParams(dimension_semantics=semantics, vmem_limit_bytes=vmem_mib * MIB)


def _gelu(x):
    return 0.5 * x * (1.0 + lax.erf(x * (2.0 ** -0.5)))


def _rmsnorm(x, g):
    return x * lax.rsqrt(jnp.mean(x * x, axis=-1, keepdims=True) + EPS) * g


def _mod_kernel(c_ref, w_ref, b_ref, o_ref):
    c = c_ref[...]
    s = (c * jax.nn.sigmoid(c)).astype(BF16)
    o_ref[...] = jnp.dot(s, w_ref[...].astype(BF16), preferred_element_type=F32) + b_ref[...]


def _modulation(c_all, w_mod, b_mod):
    rows = c_all.shape[0]
    nblk = w_mod.shape[1] // D_MODEL
    return pl.pallas_call(
        _mod_kernel,
        grid=(nblk,),
        in_specs=[pl.BlockSpec((rows, D_MODEL), lambda j: (0, 0)),
                  pl.BlockSpec((D_MODEL, D_MODEL), lambda j: (0, j)),
                  pl.BlockSpec((1, D_MODEL), lambda j: (0, j))],
        out_specs=pl.BlockSpec((rows, D_MODEL), lambda j: (0, j)),
        out_shape=jax.ShapeDtypeStruct((rows, w_mod.shape[1]), F32),
        compiler_params=_params(("arbitrary",), 32),
    )(c_all, w_mod, b_mod.reshape(1, -1))


def _rope(x, cos, sin_signed):
    width = x.shape[1]
    reps = width // LANES
    cosf = jnp.concatenate([cos] * reps, axis=1)
    sinf = jnp.concatenate([sin_signed] * reps, axis=1)
    half = HEAD_DIM // 2
    upper = pltpu.roll(x, width - half, axis=1)
    lower = pltpu.roll(x, half, axis=1)
    lane = lax.broadcasted_iota(jnp.int32, x.shape, 1)
    rot = jnp.where((lane & half) == 0, upper, lower)
    return x * cosf + rot * sinf


def _inproj_kernel(x_ref, sh_ref, sc_ref, g_ref, cos_ref, sin_ref, w_ref,
                   xr_ref, gr_ref, q_ref, k_ref, v_ref, gt_ref):
    x = x_ref[...]
    hn = (_rmsnorm(x, g_ref[...]) * (1.0 + sc_ref[0]) + sh_ref[0]).astype(BF16)

    def proj(c0, width):
        return jnp.dot(hn, w_ref[:, c0:c0 + width], preferred_element_type=F32)

    c = 0
    xr_ref[...] = proj(c, D_RNN); c += D_RNN
    gr_ref[...] = proj(c, D_RNN); c += D_RNN
    q_ref[...] = _rope(proj(c, Q_W), cos_ref[...], sin_ref[...]).astype(BF16); c += Q_W
    k_ref[...] = _rope(proj(c, KV_W), cos_ref[...], sin_ref[...]); c += KV_W
    v_ref[...] = proj(c, KV_W); c += KV_W
    gt_ref[...] = proj(c, GATE_W)


def _inproj(x2d, sh, sc, g, cos, sin_signed, w_in_bf16, tile, tiles_per_row, rope_tiles):
    n = x2d.shape[0]
    rmod = sh.shape[1]
    d_in = w_in_bf16.shape[1]
    tok = lambda width: pl.BlockSpec((tile, width), lambda i: (i, 0))
    mod = pl.BlockSpec((1, rmod, D_MODEL), lambda i: (i // tiles_per_row, 0, 0))
    rope = pl.BlockSpec((tile, LANES), lambda i: (i % rope_tiles, 0))
    return pl.pallas_call(
        _inproj_kernel,
        grid=(n // tile,),
        in_specs=[tok(D_MODEL), mod, mod, pl.BlockSpec((1, D_MODEL), lambda i: (0, 0)), rope, rope,
                  pl.BlockSpec((D_MODEL, d_in), lambda i: (0, 0), pipeline_mode=pl.Buffered(1))],
        out_specs=[tok(D_RNN), tok(D_RNN), tok(Q_W), tok(KV_W), tok(KV_W), tok(GATE_W)],
        out_shape=[jax.ShapeDtypeStruct((n, D_RNN), F32), jax.ShapeDtypeStruct((n, D_RNN), F32),
                   jax.ShapeDtypeStruct((n, Q_W), BF16), jax.ShapeDtypeStruct((n, KV_W), F32),
                   jax.ShapeDtypeStruct((n, KV_W), F32), jax.ShapeDtypeStruct((n, GATE_W), F32)],
        compiler_params=_params(("arbitrary",), 48),
    )(x2d, sh, sc, g, cos, sin_signed, w_in_bf16)


def _log_sigmoid(x):
    return -(jnp.maximum(-x, 0.0) + jnp.log1p(jnp.exp(-jnp.abs(x))))


def _lru_coeffs(xc, wa_ref, ba_ref, wi_ref, bi_ref, lam_ref):
    xb = xc.astype(BF16)
    r = jax.nn.sigmoid(jnp.dot(xb, wa_ref[...], preferred_element_type=F32) + ba_ref[...])
    i = jax.nn.sigmoid(jnp.dot(xb, wi_ref[...], preferred_element_type=F32) + bi_ref[...])
    log_a = LRU_C * r * _log_sigmoid(lam_ref[...])
    a = jnp.exp(log_a)
    mult = jnp.sqrt(-jnp.tanh(log_a) * (1.0 + a * a))
    return a, mult * (i * xc)


def _shift_rows(x, s, fill):
    rows = x.shape[0]
    if s % SUBLANES == 0:
        return jnp.concatenate([jnp.full((s, x.shape[1]), fill, x.dtype), x[:rows - s]], axis=0)
    rolled = pltpu.roll(x, s, axis=0)
    row = lax.broadcasted_iota(jnp.int32, x.shape, 0)
    return jnp.where(row < s, fill, rolled)


def _rnn_prompt_kernel(xr_ref, gr_ref, cw_ref, cb_ref, wa_ref, ba_ref, wi_ref, bi_ref, lam_ref,
                       hg_ref, hlast_ref, tail_ref, hc_ref):
    t = pl.program_id(1)
    tile = xr_ref.shape[0]

    @pl.when(t == 0)
    def _():
        tail_ref[...] = jnp.zeros_like(tail_ref)
        hc_ref[...] = jnp.zeros_like(hc_ref)

    xr = xr_ref[...]
    cat = jnp.concatenate([tail_ref[...], xr], axis=0)
    xc = cb_ref[...] + cw_ref[CONV_W - 1:CONV_W, :] * xr
    for s in range(1, CONV_W):
        shifted = pltpu.roll(cat, s, axis=0)[SUBLANES:, :]
        xc = xc + cw_ref[CONV_W - 1 - s:CONV_W - s, :] * shifted
    tail_ref[...] = xr[tile - SUBLANES:, :]

    a, b = _lru_coeffs(xc, wa_ref, ba_ref, wi_ref, bi_ref, lam_ref)
    s = 1
    while s < tile:
        a_prev = _shift_rows(a, s, 1.0)
        b_prev = _shift_rows(b, s, 0.0)
        b = a * b_prev + b
        a = a * a_prev
        s *= 2
    h = a * hc_ref[0:1, :] + b
    hlast = h[tile - 1:tile, :]
    hc_ref[...] = jnp.broadcast_to(hlast, hc_ref.shape)
    hlast_ref[0] = hlast
    hg_ref[...] = (h * _gelu(gr_ref[...])).astype(BF16)


def _rnn_prompt(xr, gr, conv_w, conv_b, wa, ba, wi, bi, lam, nb, seq, tile):
    n = xr.shape[0]
    tpr = seq // tile
    tok = pl.BlockSpec((tile, D_RNN), lambda b, t: (b * tpr + t, 0))
    row = pl.BlockSpec((1, D_RNN), lambda b, t: (0, 0))
    sq = pl.BlockSpec((D_RNN, D_RNN), lambda b, t: (0, 0))
    return pl.pallas_call(
        _rnn_prompt_kernel,
        grid=(nb, tpr),
        in_specs=[tok, tok, pl.BlockSpec((CONV_W, D_RNN), lambda b, t: (0, 0)), row, sq, row, sq, row, row],
        out_specs=[tok, pl.BlockSpec((1, 1, D_RNN), lambda b, t: (b, 0, 0))],
        out_shape=[jax.ShapeDtypeStruct((n, D_RNN), BF16), jax.ShapeDtypeStruct((nb, 1, D_RNN), F32)],
        scratch_shapes=[pltpu.VMEM((SUBLANES, D_RNN), F32), pltpu.VMEM((SUBLANES, D_RNN), F32)],
        compiler_params=_params(("arbitrary", "arbitrary"), 48),
    )(xr, gr, conv_w, conv_b, wa, ba, wi, bi, lam)


def _rnn_sample_kernel(xr_ref, gr_ref, c0_ref, c1_ref, c2_ref, h0_ref, cw_ref, cb_ref,
                       wa_ref, ba_ref, wi_ref, bi_ref, lam_ref, hg_ref, h_ref):
    xc = (cb_ref[...] + cw_ref[0:1, :] * c0_ref[...] + cw_ref[1:2, :] * c1_ref[...]
          + cw_ref[2:3, :] * c2_ref[...] + cw_ref[3:4, :] * xr_ref[...])
    a, b = _lru_coeffs(xc, wa_ref, ba_ref, wi_ref, bi_ref, lam_ref)
    h = b + a * h0_ref[...]
    h_ref[...] = h
    hg_ref[...] = (h * _gelu(gr_ref[...])).astype(BF16)


def _rnn_sample(xr, gr, c0, c1, c2, h0, conv_w, conv_b, wa, ba, wi, bi, lam):
    n = xr.shape[0]
    return pl.pallas_call(
        _rnn_sample_kernel,
        out_shape=[jax.ShapeDtypeStruct((n, D_RNN), BF16), jax.ShapeDtypeStruct((n, D_RNN), F32)],
        compiler_params=pltpu.CompilerParams(vmem_limit_bytes=48 * MIB),
    )(xr, gr, c0, c1, c2, h0, conv_w, conv_b, wa, ba, wi, bi, lam)


def _attn_prompt_kernel(sink_ref, q_ref, kp_ref, kc_ref, vp_ref, vc_ref, o_ref):
    j = pl.program_id(1)
    kk = jnp.concatenate([kp_ref[...], kc_ref[...]], axis=0).astype(BF16)
    vv = jnp.concatenate([vp_ref[...], vc_ref[...]], axis=0).astype(BF16)
    qi = lax.broadcasted_iota(jnp.int32, (WINDOW, 2 * WINDOW), 0)
    ci = lax.broadcasted_iota(jnp.int32, (WINDOW, 2 * WINDOW), 1)
    dist = qi + WINDOW - ci
    kpos = ci + (j - 1) * WINDOW
    mask = (dist >= 0) & (dist <= WINDOW) & (kpos >= 0)
    q = q_ref[...]
    outs = []
    for h in range(N_HEADS):
        g = h // GROUP
        qh = q[:, h * HEAD_DIM:(h + 1) * HEAD_DIM]
        kh = kk[:, g * HEAD_DIM:(g + 1) * HEAD_DIM]
        vh = vv[:, g * HEAD_DIM:(g + 1) * HEAD_DIM]
        s = lax.dot_general(qh, kh, (((1,), (1,)), ((), ())), preferred_element_type=F32) * (HEAD_DIM ** -0.5)
        s = jnp.where(mask, s, NEG)
        sink = sink_ref[h]
        m = jnp.maximum(jnp.max(s, axis=-1, keepdims=True), sink)
        e = jnp.exp(s - m)
        p = e / (jnp.sum(e, axis=-1, keepdims=True) + jnp.exp(sink - m))
        outs.append(jnp.dot(p.astype(BF16), vh, preferred_element_type=F32))
    o_ref[...] = jnp.concatenate(outs, axis=1).astype(BF16)


def _attn_prompt(sinks, q, k, v, nb, seq):
    n = q.shape[0]
    nblk = seq // WINDOW
    cur = lambda b, j: (b * nblk + j, 0)
    prev = lambda b, j: (b * nblk + jnp.maximum(j - 1, 0), 0)
    return pl.pallas_call(
        _attn_prompt_kernel,
        grid=(nb, nblk),
        in_specs=[pl.BlockSpec(memory_space=pltpu.SMEM),
                  pl.BlockSpec((WINDOW, Q_W), cur),
                  pl.BlockSpec((WINDOW, KV_W), prev), pl.BlockSpec((WINDOW, KV_W), cur),
                  pl.BlockSpec((WINDOW, KV_W), prev), pl.BlockSpec((WINDOW, KV_W), cur)],
        out_specs=pl.BlockSpec((WINDOW, Q_W), cur),
        out_shape=jax.ShapeDtypeStruct((n, Q_W), BF16),
        compiler_params=_params(("arbitrary", "arbitrary"), 32),
    )(sinks, q, k, k, v, v)


SAMPLE_ATTN_ROWS = 8


def _attn_sample_kernel(sink_ref, qx_ref, ck_ref, cv_ref, kn_ref, vn_ref, o_ref):
    sink = sink_ref[...]
    for r in range(SAMPLE_ATTN_ROWS):
        qx = qx_ref[r]
        kb = ck_ref[r].astype(BF16)
        vb = cv_ref[r].astype(BF16)
        kn = kn_ref[r].astype(BF16).astype(F32)
        vn = vn_ref[r].astype(BF16).astype(F32)
        scale = HEAD_DIM ** -0.5
        s = lax.dot_general(qx, kb, (((1,), (1,)), ((), ())), preferred_element_type=F32) * scale
        s_new = jnp.sum(qx.astype(F32) * kn, axis=-1, keepdims=True) * scale
        m = jnp.maximum(jnp.maximum(jnp.max(s, axis=-1, keepdims=True), s_new), sink)
        e = jnp.exp(s - m)
        e_new = jnp.exp(s_new - m)
        den = jnp.sum(e, axis=-1, keepdims=True) + e_new + jnp.exp(sink - m)
        p = (e / den).astype(BF16)
        p_new = (e_new / den).astype(BF16).astype(F32)
        o_ref[r] = jnp.dot(p, vb, preferred_element_type=F32) + p_new * vn


def _attn_sample(sinks_col, qx, cache_k, cache_v, k_new, v_new):
    nb, wbuf = cache_k.shape[0], cache_k.shape[1]
    rows = SAMPLE_ATTN_ROWS
    blk = lambda d1, d2: pl.BlockSpec((rows, d1, d2), lambda i: (i, 0, 0))
    return pl.pallas_call(
        _attn_sample_kernel,
        grid=(nb // rows,),
        in_specs=[pl.BlockSpec((N_HEADS, 1), lambda i: (0, 0)), blk(N_HEADS, KV_W),
                  blk(wbuf, KV_W), blk(wbuf, KV_W), blk(1, KV_W), blk(1, KV_W)],
        out_specs=blk(N_HEADS, KV_W),
        out_shape=jax.ShapeDtypeStruct((nb, N_HEADS, KV_W), F32),
        compiler_params=_params(("arbitrary",), 32),
    )(sinks_col, qx, cache_k, cache_v, k_new, v_new)


def _merge_kernel(hg_ref, o_ref, gt_ref, x_ref, g1_ref, sh_ref, sc_ref, gn_ref, wr_ref, wa_ref, wo_ref,
                  x1_ref, h2t_ref):
    y_r = jnp.dot(hg_ref[...], wr_ref[...], preferred_element_type=F32)
    y_a = jnp.dot(o_ref[...], wa_ref[...], preferred_element_type=F32)
    merged = (jax.nn.sigmoid(gt_ref[:, :D_MODEL]) * y_r + jax.nn.sigmoid(gt_ref[:, D_MODEL:]) * y_a)
    x1 = x_ref[...] + g1_ref[0] * jnp.dot(merged.astype(BF16), wo_ref[...], preferred_element_type=F32)
    x1_ref[...] = x1
    h2 = _rmsnorm(x1, gn_ref[...]) * (1.0 + sc_ref[0]) + sh_ref[0]
    h2t_ref[...] = h2.T.astype(BF16)


def _merge(hg, o, gates, x2d, gt1, sh2, sc2, g2, wr, wa, wo, tile, tiles_per_row):
    n = x2d.shape[0]
    rmod = gt1.shape[1]
    tok = lambda width: pl.BlockSpec((tile, width), lambda i: (i, 0))
    mod = pl.BlockSpec((1, rmod, D_MODEL), lambda i: (i // tiles_per_row, 0, 0))
    full = lambda a: pl.BlockSpec(a.shape, lambda i: (0, 0))
    return pl.pallas_call(
        _merge_kernel,
        grid=(n // tile,),
        in_specs=[tok(D_RNN), tok(Q_W), tok(GATE_W), tok(D_MODEL), mod, mod, mod,
                  pl.BlockSpec((1, D_MODEL), lambda i: (0, 0)), full(wr), full(wa), full(wo)],
        out_specs=[tok(D_MODEL), pl.BlockSpec((D_MODEL, tile), lambda i: (0, i))],
        out_shape=[jax.ShapeDtypeStruct((n, D_MODEL), F32), jax.ShapeDtypeStruct((D_MODEL, n), BF16)],
        compiler_params=_params(("arbitrary",), 48),
    )(hg, o, gates, x2d, gt1, sh2, sc2, g2, wr, wa, wo)


def _vmax(a, b):
    if a is None:
        return b
    if b is None:
        return a
    return jnp.maximum(a, b)


def _vmin(a, b):
    if a is None or b is None:
        return None
    return jnp.minimum(a, b)


def _compare_exchange(xs, i, j):
    hi, lo = _vmax(xs[i], xs[j]), _vmin(xs[i], xs[j])
    xs[i], xs[j] = hi, lo


def _sort_pairs(n):
    pairs = []

    def merge(lo, hi, r):
        step = r * 2
        if step < hi - lo:
            merge(lo, hi, step)
            merge(lo + r, hi, step)
            for i in range(lo + r, hi - r, step):
                pairs.append((i, i + r))
        else:
            pairs.append((lo, lo + r))

    def sort(lo, hi):
        if hi - lo >= 1:
            mid = lo + (hi - lo) // 2
            sort(lo, mid)
            sort(mid + 1, hi)
            merge(lo, hi, 1)

    sort(0, n - 1)
    return pairs


_SORT16 = _sort_pairs(PEER_TOPK)


def _sort_desc(xs):
    xs = list(xs)
    for i, j in _SORT16:
        _compare_exchange(xs, i, j)
    return xs


def _bitonic_to_desc(xs):
    xs = list(xs)
    d = len(xs) // 2
    while d >= 1:
        for i in range(len(xs)):
            if (i & d) == 0:
                _compare_exchange(xs, i, i + d)
        d //= 2
    return xs


def _top_merge(xs, ys):
    k = len(xs)
    return _bitonic_to_desc([_vmax(xs[i], ys[k - 1 - i]) for i in range(k)])


def _top16_rows(s):
    rows = _sort_desc([s[r * SUBLANES:(r + 1) * SUBLANES, :] for r in range(N_KEYS // SUBLANES)])
    shift = SUBLANES // 2
    while shift >= 1:
        rows = _top_merge(rows, [pltpu.roll(x, shift, axis=0) for x in rows])
        shift //= 2
    return rows


def _top16_pair_sums(v1, v2):
    k = PEER_TOPK
    col = lambda j: [v1[i] + v2[j] for i in range(k // (j + 1))]
    g0 = col(0)
    row0_tail = [v1[0] + v2[j] for j in range(k // 2, k)]
    g1 = _bitonic_to_desc(col(1) + row0_tail[::-1])
    g2 = _sort_desc(col(2) + col(3) + col(4) + col(5) + col(6))
    g3 = col(7) + [None] * (k - 2)
    return _top_merge(_top_merge(g0, g1), _top_merge(g2, g3))


def _route_kernel(h2t_ref, wq_ref, keys_ref, s1_ref, c1_ref, s2_ref, e2_ref, tau_ref, qt_ref):
    qt_ref[...] = jnp.dot(wq_ref[...], h2t_ref[...], preferred_element_type=F32)

    def head(h, carry):
        base = pl.multiple_of(h * PEER_DK, PEER_DK)
        q1 = qt_ref[pl.ds(base, PEER_DHALF), :].astype(BF16)
        q2 = qt_ref[pl.ds(base + PEER_DHALF, PEER_DHALF), :].astype(BF16)
        s1 = jnp.dot(keys_ref[h, 0], q1, preferred_element_type=F32)
        s2 = jnp.dot(keys_ref[h, 1], q2, preferred_element_type=F32)
        v1 = _top16_rows(s1)
        v2 = _top16_rows(s2)
        top = _top16_pair_sums(v1, v2)
        z = jnp.ones_like(top[0])
        for f in top[1:]:
            z = z + jnp.exp(f - top[0])
        s1_ref[h] = s1
        s2_ref[h] = s2
        c1_ref[h] = jnp.exp(s1 - v1[0][0:1, :]) * (1.0 / z[0:1, :])
        e2_ref[h] = jnp.exp(s2 - v2[0][0:1, :])
        tau_ref[h] = top[PEER_TOPK - 1]
        return carry

    lax.fori_loop(0, PEER_HEADS, head, 0)


def _route(h2t, wq_t, keys_bf16, tile):
    n = h2t.shape[1]
    tab = pl.BlockSpec((PEER_HEADS, N_KEYS, tile), lambda i: (0, 0, i))
    tab_shape = jax.ShapeDtypeStruct((PEER_HEADS, N_KEYS, n), F32)
    return pl.pallas_call(
        _route_kernel,
        grid=(n // tile,),
        in_specs=[pl.BlockSpec((D_MODEL, tile), lambda i: (0, i)),
                  pl.BlockSpec(wq_t.shape, lambda i: (0, 0)),
                  pl.BlockSpec(keys_bf16.shape, lambda i: (0, 0, 0, 0))],
        out_specs=[tab, tab, tab, tab, pl.BlockSpec((PEER_HEADS, SUBLANES, tile), lambda i: (0, 0, i))],
        out_shape=[tab_shape, tab_shape, tab_shape, tab_shape,
                   jax.ShapeDtypeStruct((PEER_HEADS, SUBLANES, n), F32)],
        scratch_shapes=[pltpu.VMEM((PEER_HEADS * PEER_DK, tile), F32)],
        compiler_params=_params(("arbitrary",), 48),
    )(h2t, wq_t, keys_bf16)


def _peer_kernel(h2t_ref, u_ref, vt_ref, s1_ref, c1_ref, s2_ref, e2_ref, tau_ref, out_ref, act_ref, wgt_ref):
    j = pl.program_id(1)
    tile = h2t_ref.shape[1]
    chunk = min(PEER_LANE_CHUNK, tile)

    @pl.when(j == 0)
    def _():
        out_ref[...] = jnp.zeros_like(out_ref)

    act_ref[...] = jnp.dot(u_ref[...], h2t_ref[...], preferred_element_type=F32)

    def lanes(c, carry):
        lo = pl.multiple_of(c * chunk, chunk)
        sl = pl.ds(lo, chunk)
        for a in range(PEER_A_PER_BLOCK):
            rows = pl.ds(a * N_KEYS, N_KEYS)
            gate = jnp.zeros((N_KEYS, chunk), F32)
            for h in range(PEER_HEADS):
                pair = s2_ref[h, :, sl] + s1_ref[h, a:a + 1, sl]
                sel = jnp.where(pair >= tau_ref[h, 0:1, sl], e2_ref[h, :, sl], 0.0)
                gate = gate + sel * c1_ref[h, a:a + 1, sl]
            wgt_ref[rows, sl] = (gate * _gelu(act_ref[rows, sl])).astype(BF16)
        return carry

    lax.fori_loop(0, tile // chunk, lanes, 0)
    out_ref[...] += jnp.dot(vt_ref[...], wgt_ref[...], preferred_element_type=F32)


def _peer(h2t, u_bf16, vt_bf16, s1, c1, s2, e2, tau, tile):
    n = h2t.shape[1]
    eb = PEER_EXPERT_BLOCK
    return pl.pallas_call(
        _peer_kernel,
        grid=(n // tile, N_EXPERTS // eb),
        in_specs=[pl.BlockSpec((D_MODEL, tile), lambda i, j: (0, i)),
                  pl.BlockSpec((eb, D_MODEL), lambda i, j: (j, 0)),
                  pl.BlockSpec((D_MODEL, eb), lambda i, j: (0, j)),
                  pl.BlockSpec((PEER_HEADS, PEER_A_PER_BLOCK, tile), lambda i, j: (0, j, i)),
                  pl.BlockSpec((PEER_HEADS, PEER_A_PER_BLOCK, tile), lambda i, j: (0, j, i)),
                  pl.BlockSpec((PEER_HEADS, N_KEYS, tile), lambda i, j: (0, 0, i)),
                  pl.BlockSpec((PEER_HEADS, N_KEYS, tile), lambda i, j: (0, 0, i)),
                  pl.BlockSpec((PEER_HEADS, SUBLANES, tile), lambda i, j: (0, 0, i))],
        out_specs=pl.BlockSpec((D_MODEL, tile), lambda i, j: (0, i)),
        out_shape=jax.ShapeDtypeStruct((D_MODEL, n), F32),
        scratch_shapes=[pltpu.VMEM((eb, tile), F32), pltpu.VMEM((eb, tile), BF16)],
        compiler_params=_params(("arbitrary", "arbitrary"), 48),
    )(h2t, u_bf16, vt_bf16, s1, c1, s2, e2, tau)


def _final_kernel(x1_ref, pt_ref, g2_ref, gf_ref, y_ref):
    x2 = x1_ref[...] + g2_ref[0] * pt_ref[...].T
    y_ref[...] = _rmsnorm(x2, gf_ref[...])


def _final(x1, peer_t, gt2, g_final, tile, tiles_per_row):
    n = x1.shape[0]
    rmod = gt2.shape[1]
    return pl.pallas_call(
        _final_kernel,
        grid=(n // tile,),
        in_specs=[pl.BlockSpec((tile, D_MODEL), lambda i: (i, 0)),
                  pl.BlockSpec((D_MODEL, tile), lambda i: (0, i)),
                  pl.BlockSpec((1, rmod, D_MODEL), lambda i: (i // tiles_per_row, 0, 0)),
                  pl.BlockSpec((1, D_MODEL), lambda i: (0, 0))],
        out_specs=pl.BlockSpec((tile, D_MODEL), lambda i: (i, 0)),
        out_shape=jax.ShapeDtypeStruct((n, D_MODEL), F32),
        compiler_params=_params(("arbitrary",), 32),
    )(x1, peer_t, gt2, g_final)


def _rope_tables(pos):
    half = HEAD_DIM // 2
    inv = ROPE_THETA ** (-jnp.arange(half, dtype=F32) / half)
    ang = pos.astype(F32)[:, None] * inv[None, :]
    cos, sin = jnp.cos(ang), jnp.sin(ang)
    reps = LANES // HEAD_DIM
    return (jnp.tile(jnp.concatenate([cos, cos], axis=1), (1, reps)),
            jnp.tile(jnp.concatenate([-sin, sin], axis=1), (1, reps)))


def _block_diag(w):
    eye = jnp.eye(RNN_BLOCKS, dtype=w.dtype)
    return (eye[:, None, :, None] * w[:, :, None, :]).reshape(D_RNN, D_RNN)


def _channel_mixer(h2t, x1, gt2, g_final, wq_t, keys, u_bf16, vt_bf16, tile, peer_tile, tiles_per_row):
    s1, c1, s2, e2, tau = _route(h2t, wq_t, keys, tile)
    peer_t = _peer(h2t, u_bf16, vt_bf16, s1, c1, s2, e2, tau, peer_tile)
    return _final(x1, peer_t, gt2, g_final, tile, tiles_per_row)


def kernel(x_prompt, x_sample, state_conv, state_rnn, cache_win_k, cache_win_v, c_prompt, c_sample, g_norm1, g_norm2, g_final, w_mod, b_mod, w_in, conv_w, conv_b, w_a, b_a, w_i, b_i, lam, w_rnn_out, sinks, w_attn_out, w_out, w_pq, sub_keys, u_tab, v_tab):
    nb, seq, _ = x_prompt.shape
    ns = x_sample.shape[0]
    wbuf = cache_win_k.shape[2]
    assert x_sample.shape[1] == 1 and g_norm1.shape[0] == 1
    assert seq % TOKEN_TILE == 0 and seq % WINDOW == 0 and ns % LANES == 0 and (nb * seq) % PEER_TOKEN_TILE == 0
    l = 0

    w_in_b = w_in[l].astype(BF16)
    wa_d = _block_diag(w_a[l]).astype(BF16)
    wi_d = _block_diag(w_i[l]).astype(BF16)
    wr_b, wat_b, wo_b = w_rnn_out[l].astype(BF16), w_attn_out[l].astype(BF16), w_out[l].astype(BF16)
    wq_t = w_pq[l].T.astype(BF16)
    keys_b = sub_keys[l].astype(BF16)
    u_b = u_tab[l].astype(BF16)
    vt_b = v_tab[l].T.astype(BF16)
    row = lambda v: v.reshape(1, -1)
    g1, g2, gf = row(g_norm1[l]), row(g_norm2[l]), row(g_final)
    cb, ba, bi, lm = row(conv_b[l]), row(b_a[l]), row(b_i[l]), row(lam[l])

    mod = _modulation(jnp.concatenate([c_prompt, c_sample], axis=0), w_mod[l], b_mod[l])
    mod_p = [m.reshape(nb, 1, D_MODEL) for m in jnp.split(mod[:nb], 6, axis=-1)]
    mod_s = [m.reshape(1, ns, D_MODEL) for m in jnp.split(mod[nb:], 6, axis=-1)]

    tile = TOKEN_TILE
    tpr = seq // tile
    xp = x_prompt.reshape(nb * seq, D_MODEL)
    cos_p, sin_p = _rope_tables(jnp.arange(seq))
    xr, gr, q, k, v, gates = _inproj(xp, mod_p[0], mod_p[1], g1, cos_p, sin_p, w_in_b, tile, tpr, tpr)
    hg, rnn_p = _rnn_prompt(xr, gr, conv_w[l], cb, wa_d, ba, wi_d, bi, lm, nb, seq, tile)
    o = _attn_prompt(sinks[l], q, k, v, nb, seq)
    x1, h2t = _merge(hg, o, gates, xp, mod_p[2], mod_p[3], mod_p[4], g2, wr_b, wat_b, wo_b, tile, tpr)
    y_p = _channel_mixer(h2t, x1, mod_p[5], gf, wq_t, keys_b, u_b, vt_b, tile, PEER_TOKEN_TILE, tpr)
    y_prompt = y_p.reshape(nb, seq, D_MODEL)
    conv_prompt = xr.reshape(nb, seq, D_RNN)[:, seq - (CONV_W - 1):][None]
    rnn_prompt = rnn_p.reshape(1, nb, D_RNN)
    nbuf = min(WINDOW, seq)
    win_k_prompt = k.reshape(nb, seq, N_KV, HEAD_DIM)[:, seq - nbuf:][None]
    win_v_prompt = v.reshape(nb, seq, N_KV, HEAD_DIM)[:, seq - nbuf:][None]

    xs = x_sample.reshape(ns, D_MODEL)
    cos_s, sin_s = _rope_tables(jnp.full((ns,), PAST_LEN))
    xr_s, gr_s, q_s, k_s, v_s, gates_s = _inproj(xs, mod_s[0], mod_s[1], g1, cos_s, sin_s, w_in_b, ns, 1, 1)
    sc = state_conv[l]
    hg_s, h_s = _rnn_sample(xr_s, gr_s, sc[:, 0], sc[:, 1], sc[:, 2], state_rnn[l],
                            conv_w[l], cb, wa_d, ba, wi_d, bi, lm)
    head_group = (jnp.arange(N_HEADS)[:, None] // GROUP == jnp.arange(N_KV)[None, :])
    qx = jnp.where(head_group[None, :, :, None], q_s.reshape(ns, N_HEADS, 1, HEAD_DIM), 0)
    qx = qx.reshape(ns, N_HEADS, KV_W).astype(BF16)
    ck = cache_win_k[l].reshape(ns, wbuf, KV_W)
    cv = cache_win_v[l].reshape(ns, wbuf, KV_W)
    ox = _attn_sample(sinks[l].reshape(N_HEADS, 1), qx, ck, cv, k_s.reshape(ns, 1, KV_W), v_s.reshape(ns, 1, KV_W))
    o_s = jnp.where(head_group[None, :, :, None], ox.reshape(ns, N_HEADS, N_KV, HEAD_DIM), 0.0).sum(axis=2)
    o_s = o_s.reshape(ns, Q_W).astype(BF16)
    x1_s, h2t_s = _merge(hg_s, o_s, gates_s, xs, mod_s[2], mod_s[3], mod_s[4], g2, wr_b, wat_b, wo_b, ns, 1)
    y_s = _channel_mixer(h2t_s, x1_s, mod_s[5], gf, wq_t, keys_b, u_b, vt_b, ns, ns, 1)
    y_sample = y_s.reshape(ns, 1, D_MODEL)
    conv_sample = jnp.concatenate([sc[:, 1:], xr_s[:, None, :]], axis=1)[None]
    rnn_sample = h_s[None]
    win_k_sample = jnp.concatenate([ck[:, 1:], k_s[:, None, :]], axis=1).reshape(1, ns, wbuf, N_KV, HEAD_DIM)
    win_v_sample = jnp.concatenate([cv[:, 1:], v_s[:, None, :]], axis=1).reshape(1, ns, wbuf, N_KV, HEAD_DIM)

    return (y_prompt, y_sample, conv_prompt, rnn_prompt, win_k_prompt, win_v_prompt,
            conv_sample, rnn_sample, win_k_sample, win_v_sample)
```

```python
import functools

import jax
import jax.numpy as jnp
from jax import lax
from jax.experimental import pallas as pl
from jax.experimental.pallas import tpu as pltpu

F32 = jnp.float32
BF16 = jnp.bfloat16

D_MODEL = 1024
PAST_LEN = 8192
D_RNN = 1280
RNN_BLOCKS = 16
RNN_BW = D_RNN // RNN_BLOCKS
CONV_W = 4
LRU_C = 8.0
N_HEADS = 16
N_KV = 4
HEAD_DIM = 64
GROUP = N_HEADS // N_KV
WINDOW = 128
ROPE_THETA = 10000.0
Q_W = N_HEADS * HEAD_DIM
KV_W = N_KV * HEAD_DIM
PEER_HEADS = 8
N_KEYS = 128
N_EXPERTS = N_KEYS * N_KEYS
PEER_DK = 256
PEER_DHALF = PEER_DK // 2
PEER_TOPK = 16
EPS = 1e-6
NEG = -1e30
GATE_W = 2 * D_MODEL

LANES = 128
SUBLANES = 8
MIB = 1024 * 1024

TOKEN_TILE = 256
PEER_TOKEN_TILE = 512
PEER_UNIT_ROWS = 32
PEER_MATMUL_LANES = 256
PEER_EXPERT_BLOCK = 1024
PEER_A_PER_BLOCK = PEER_EXPERT_BLOCK // N_KEYS


def _params(semantics, vmem_mib):
    return pltpu.CompilerParams(dimension_semantics=semantics, vmem_limit_bytes=vmem_mib * MIB)


def _gelu(x):
    return 0.5 * x * (1.0 + lax.erf(x * (2.0 ** -0.5)))


def _rmsnorm(x, g):
    return x * lax.rsqrt(jnp.mean(x * x, axis=-1, keepdims=True) + EPS) * g


def _mod_kernel(c_ref, w_ref, b_ref, o_ref):
    c = c_ref[...]
    s = (c * jax.nn.sigmoid(c)).astype(BF16)
    o_ref[...] = jnp.dot(s, w_ref[...].astype(BF16), preferred_element_type=F32) + b_ref[...]


def _modulation(c_all, w_mod, b_mod):
    rows = c_all.shape[0]
    nblk = w_mod.shape[1] // D_MODEL
    return pl.pallas_call(
        _mod_kernel,
        grid=(nblk,),
        in_specs=[pl.BlockSpec((rows, D_MODEL), lambda j: (0, 0)),
                  pl.BlockSpec((D_MODEL, D_MODEL), lambda j: (0, j)),
                  pl.BlockSpec((1, D_MODEL), lambda j: (0, j))],
        out_specs=pl.BlockSpec((rows, D_MODEL), lambda j: (0, j)),
        out_shape=jax.ShapeDtypeStruct((rows, w_mod.shape[1]), F32),
        compiler_params=_params(("arbitrary",), 32),
    )(c_all, w_mod, b_mod.reshape(1, -1))


def _rope(x, cos, sin_signed):
    width = x.shape[1]
    reps = width // LANES
    cosf = jnp.concatenate([cos] * reps, axis=1)
    sinf = jnp.concatenate([sin_signed] * reps, axis=1)
    half = HEAD_DIM // 2
    upper = pltpu.roll(x, width - half, axis=1)
    lower = pltpu.roll(x, half, axis=1)
    lane = lax.broadcasted_iota(jnp.int32, x.shape, 1)
    rot = jnp.where((lane & half) == 0, upper, lower)
    return x * cosf + rot * sinf


def _inproj_kernel(x_ref, sh_ref, sc_ref, g_ref, cos_ref, sin_ref, w_ref,
                   xr_ref, gr_ref, q_ref, k_ref, v_ref, gt_ref):
    x = x_ref[...]
    hn = (_rmsnorm(x, g_ref[...]) * (1.0 + sc_ref[0]) + sh_ref[0]).astype(BF16)

    def proj(c0, width):
        return jnp.dot(hn, w_ref[:, c0:c0 + width], preferred_element_type=F32)

    c = 0
    xr_ref[...] = proj(c, D_RNN); c += D_RNN
    gr_ref[...] = proj(c, D_RNN); c += D_RNN
    q_ref[...] = _rope(proj(c, Q_W), cos_ref[...], sin_ref[...]).astype(BF16); c += Q_W
    k_ref[...] = _rope(proj(c, KV_W), cos_ref[...], sin_ref[...]); c += KV_W
    v_ref[...] = proj(c, KV_W); c += KV_W
    gt_ref[...] = proj(c, GATE_W)


def _inproj(x2d, sh, sc, g, cos, sin_signed, w_in_bf16, tile, tiles_per_row, rope_tiles):
    n = x2d.shape[0]
    rmod = sh.shape[1]
    d_in = w_in_bf16.shape[1]
    tok = lambda width: pl.BlockSpec((tile, width), lambda i: (i, 0))
    mod = pl.BlockSpec((1, rmod, D_MODEL), lambda i: (i // tiles_per_row, 0, 0))
    rope = pl.BlockSpec((tile, LANES), lambda i: (i % rope_tiles, 0))
    return pl.pallas_call(
        _inproj_kernel,
        grid=(n // tile,),
        in_specs=[tok(D_MODEL), mod, mod, pl.BlockSpec((1, D_MODEL), lambda i: (0, 0)), rope, rope,
                  pl.BlockSpec((D_MODEL, d_in), lambda i: (0, 0), pipeline_mode=pl.Buffered(1))],
        out_specs=[tok(D_RNN), tok(D_RNN), tok(Q_W), tok(KV_W), tok(KV_W), tok(GATE_W)],
        out_shape=[jax.ShapeDtypeStruct((n, D_RNN), F32), jax.ShapeDtypeStruct((n, D_RNN), F32),
                   jax.ShapeDtypeStruct((n, Q_W), BF16), jax.ShapeDtypeStruct((n, KV_W), F32),
                   jax.ShapeDtypeStruct((n, KV_W), F32), jax.ShapeDtypeStruct((n, GATE_W), F32)],
        compiler_params=_params(("arbitrary",), 48),
    )(x2d, sh, sc, g, cos, sin_signed, w_in_bf16)


def _log_sigmoid(x):
    return -(jnp.maximum(-x, 0.0) + jnp.log1p(jnp.exp(-jnp.abs(x))))


def _lru_coeffs(xc, wa_ref, ba_ref, wi_ref, bi_ref, lam_ref):
    xb = xc.astype(BF16)
    r = jax.nn.sigmoid(jnp.dot(xb, wa_ref[...], preferred_element_type=F32) + ba_ref[...])
    i = jax.nn.sigmoid(jnp.dot(xb, wi_ref[...], preferred_element_type=F32) + bi_ref[...])
    log_a = LRU_C * r * _log_sigmoid(lam_ref[...])
    a = jnp.exp(log_a)
    mult = jnp.sqrt(-jnp.tanh(log_a) * (1.0 + a * a))
    return a, mult * (i * xc)


def _shift_rows(x, s, fill):
    rows = x.shape[0]
    if s % SUBLANES == 0:
        return jnp.concatenate([jnp.full((s, x.shape[1]), fill, x.dtype), x[:rows - s]], axis=0)
    rolled = pltpu.roll(x, s, axis=0)
    row = lax.broadcasted_iota(jnp.int32, x.shape, 0)
    return jnp.where(row < s, fill, rolled)


def _rnn_prompt_kernel(xr_ref, gr_ref, cw_ref, cb_ref, wa_ref, ba_ref, wi_ref, bi_ref, lam_ref,
                       hg_ref, hlast_ref, tail_ref, hc_ref):
    t = pl.program_id(1)
    tile = xr_ref.shape[0]

    @pl.when(t == 0)
    def _():
        tail_ref[...] = jnp.zeros_like(tail_ref)
        hc_ref[...] = jnp.zeros_like(hc_ref)

    xr = xr_ref[...]
    cat = jnp.concatenate([tail_ref[...], xr], axis=0)
    xc = cb_ref[...] + cw_ref[CONV_W - 1:CONV_W, :] * xr
    for s in range(1, CONV_W):
        shifted = pltpu.roll(cat, s, axis=0)[SUBLANES:, :]
        xc = xc + cw_ref[CONV_W - 1 - s:CONV_W - s, :] * shifted
    tail_ref[...] = xr[tile - SUBLANES:, :]

    a, b = _lru_coeffs(xc, wa_ref, ba_ref, wi_ref, bi_ref, lam_ref)
    s = 1
    while s < tile:
        a_prev = _shift_rows(a, s, 1.0)
        b_prev = _shift_rows(b, s, 0.0)
        b = a * b_prev + b
        a = a * a_prev
        s *= 2
    h = a * hc_ref[0:1, :] + b
    hlast = h[tile - 1:tile, :]
    hc_ref[...] = jnp.broadcast_to(hlast, hc_ref.shape)
    hlast_ref[0] = hlast
    hg_ref[...] = (h * _gelu(gr_ref[...])).astype(BF16)


def _rnn_prompt(xr, gr, conv_w, conv_b, wa, ba, wi, bi, lam, nb, seq, tile):
    n = xr.shape[0]
    tpr = seq // tile
    tok = pl.BlockSpec((tile, D_RNN), lambda b, t: (b * tpr + t, 0))
    row = pl.BlockSpec((1, D_RNN), lambda b, t: (0, 0))
    sq = pl.BlockSpec((D_RNN, D_RNN), lambda b, t: (0, 0))
    return pl.pallas_call(
        _rnn_prompt_kernel,
        grid=(nb, tpr),
        in_specs=[tok, tok, pl.BlockSpec((CONV_W, D_RNN), lambda b, t: (0, 0)), row, sq, row, sq, row, row],
        out_specs=[tok, pl.BlockSpec((1, 1, D_RNN), lambda b, t: (b, 0, 0))],
        out_shape=[jax.ShapeDtypeStruct((n, D_RNN), BF16), jax.ShapeDtypeStruct((nb, 1, D_RNN), F32)],
        scratch_shapes=[pltpu.VMEM((SUBLANES, D_RNN), F32), pltpu.VMEM((SUBLANES, D_RNN), F32)],
        compiler_params=_params(("arbitrary", "arbitrary"), 48),
    )(xr, gr, conv_w, conv_b, wa, ba, wi, bi, lam)


def _rnn_sample_kernel(xr_ref, gr_ref, c0_ref, c1_ref, c2_ref, h0_ref, cw_ref, cb_ref,
                       wa_ref, ba_ref, wi_ref, bi_ref, lam_ref, hg_ref, h_ref):
    xc = (cb_ref[...] + cw_ref[0:1, :] * c0_ref[...] + cw_ref[1:2, :] * c1_ref[...]
          + cw_ref[2:3, :] * c2_ref[...] + cw_ref[3:4, :] * xr_ref[...])
    a, b = _lru_coeffs(xc, wa_ref, ba_ref, wi_ref, bi_ref, lam_ref)
    h = b + a * h0_ref[...]
    h_ref[...] = h
    hg_ref[...] = (h * _gelu(gr_ref[...])).astype(BF16)


def _rnn_sample(xr, gr, c0, c1, c2, h0, conv_w, conv_b, wa, ba, wi, bi, lam):
    n = xr.shape[0]
    return pl.pallas_call(
        _rnn_sample_kernel,
        out_shape=[jax.ShapeDtypeStruct((n, D_RNN), BF16), jax.ShapeDtypeStruct((n, D_RNN), F32)],
        compiler_params=pltpu.CompilerParams(vmem_limit_bytes=48 * MIB),
    )(xr, gr, c0, c1, c2, h0, conv_w, conv_b, wa, ba, wi, bi, lam)


def _attn_prompt_kernel(sink_ref, q_ref, kp_ref, kc_ref, vp_ref, vc_ref, o_ref):
    j = pl.program_id(1)
    kk = jnp.concatenate([kp_ref[...], kc_ref[...]], axis=0).astype(BF16)
    vv = jnp.concatenate([vp_ref[...], vc_ref[...]], axis=0).astype(BF16)
    qi = lax.broadcasted_iota(jnp.int32, (WINDOW, 2 * WINDOW), 0)
    ci = lax.broadcasted_iota(jnp.int32, (WINDOW, 2 * WINDOW), 1)
    dist = qi + WINDOW - ci
    kpos = ci + (j - 1) * WINDOW
    mask = (dist >= 0) & (dist <= WINDOW) & (kpos >= 0)
    q = q_ref[...]
    outs = []
    for h in range(N_HEADS):
        g = h // GROUP
        qh = q[:, h * HEAD_DIM:(h + 1) * HEAD_DIM]
        kh = kk[:, g * HEAD_DIM:(g + 1) * HEAD_DIM]
        vh = vv[:, g * HEAD_DIM:(g + 1) * HEAD_DIM]
        s = lax.dot_general(qh, kh, (((1,), (1,)), ((), ())), preferred_element_type=F32) * (HEAD_DIM ** -0.5)
        s = jnp.where(mask, s, NEG)
        sink = sink_ref[h]
        m = jnp.maximum(jnp.max(s, axis=-1, keepdims=True), sink)
        e = jnp.exp(s - m)
        p = e / (jnp.sum(e, axis=-1, keepdims=True) + jnp.exp(sink - m))
        outs.append(jnp.dot(p.astype(BF16), vh, preferred_element_type=F32))
    o_ref[...] = jnp.concatenate(outs, axis=1).astype(BF16)


def _attn_prompt(sinks, q, k, v, nb, seq):
    n = q.shape[0]
    nblk = seq // WINDOW
    cur = lambda b, j: (b * nblk + j, 0)
    prev = lambda b, j: (b * nblk + jnp.maximum(j - 1, 0), 0)
    return pl.pallas_call(
        _attn_prompt_kernel,
        grid=(nb, nblk),
        in_specs=[pl.BlockSpec(memory_space=pltpu.SMEM),
                  pl.BlockSpec((WINDOW, Q_W), cur),
                  pl.BlockSpec((WINDOW, KV_W), prev), pl.BlockSpec((WINDOW, KV_W), cur),
                  pl.BlockSpec((WINDOW, KV_W), prev), pl.BlockSpec((WINDOW, KV_W), cur)],
        out_specs=pl.BlockSpec((WINDOW, Q_W), cur),
        out_shape=jax.ShapeDtypeStruct((n, Q_W), BF16),
        compiler_params=_params(("arbitrary", "arbitrary"), 32),
    )(sinks, q, k, k, v, v)


SAMPLE_ATTN_ROWS = 8


def _attn_sample_kernel(sink_ref, qx_ref, ck_ref, cv_ref, kn_ref, vn_ref, o_ref):
    sink = sink_ref[...]
    for r in range(SAMPLE_ATTN_ROWS):
        qx = qx_ref[r]
        kb = ck_ref[r].astype(BF16)
        vb = cv_ref[r].astype(BF16)
        kn = kn_ref[r].astype(BF16).astype(F32)
        vn = vn_ref[r].astype(BF16).astype(F32)
        scale = HEAD_DIM ** -0.5
        s = lax.dot_general(qx, kb, (((1,), (1,)), ((), ())), preferred_element_type=F32) * scale
        s_new = jnp.sum(qx.astype(F32) * kn, axis=-1, keepdims=True) * scale
        m = jnp.maximum(jnp.maximum(jnp.max(s, axis=-1, keepdims=True), s_new), sink)
        e = jnp.exp(s - m)
        e_new = jnp.exp(s_new - m)
        den = jnp.sum(e, axis=-1, keepdims=True) + e_new + jnp.exp(sink - m)
        p = (e / den).astype(BF16)
        p_new = (e_new / den).astype(BF16).astype(F32)
        o_ref[r] = jnp.dot(p, vb, preferred_element_type=F32) + p_new * vn


def _attn_sample(sinks_col, qx, cache_k, cache_v, k_new, v_new):
    nb, wbuf = cache_k.shape[0], cache_k.shape[1]
    rows = SAMPLE_ATTN_ROWS
    blk = lambda d1, d2: pl.BlockSpec((rows, d1, d2), lambda i: (i, 0, 0))
    return pl.pallas_call(
        _attn_sample_kernel,
        grid=(nb // rows,),
        in_specs=[pl.BlockSpec((N_HEADS, 1), lambda i: (0, 0)), blk(N_HEADS, KV_W),
                  blk(wbuf, KV_W), blk(wbuf, KV_W), blk(1, KV_W), blk(1, KV_W)],
        out_specs=blk(N_HEADS, KV_W),
        out_shape=jax.ShapeDtypeStruct((nb, N_HEADS, KV_W), F32),
        compiler_params=_params(("arbitrary",), 32),
    )(sinks_col, qx, cache_k, cache_v, k_new, v_new)


def _merge_kernel(hg_ref, o_ref, gt_ref, x_ref, g1_ref, sh_ref, sc_ref, gn_ref, wr_ref, wa_ref, wo_ref,
                  x1_ref, h2t_ref):
    y_r = jnp.dot(hg_ref[...], wr_ref[...], preferred_element_type=F32)
    y_a = jnp.dot(o_ref[...], wa_ref[...], preferred_element_type=F32)
    merged = (jax.nn.sigmoid(gt_ref[:, :D_MODEL]) * y_r + jax.nn.sigmoid(gt_ref[:, D_MODEL:]) * y_a)
    x1 = x_ref[...] + g1_ref[0] * jnp.dot(merged.astype(BF16), wo_ref[...], preferred_element_type=F32)
    x1_ref[...] = x1
    h2 = _rmsnorm(x1, gn_ref[...]) * (1.0 + sc_ref[0]) + sh_ref[0]
    h2t_ref[...] = h2.T.astype(BF16)


def _merge(hg, o, gates, x2d, gt1, sh2, sc2, g2, wr, wa, wo, tile, tiles_per_row):
    n = x2d.shape[0]
    rmod = gt1.shape[1]
    tok = lambda width: pl.BlockSpec((tile, width), lambda i: (i, 0))
    mod = pl.BlockSpec((1, rmod, D_MODEL), lambda i: (i // tiles_per_row, 0, 0))
    full = lambda a: pl.BlockSpec(a.shape, lambda i: (0, 0))
    return pl.pallas_call(
        _merge_kernel,
        grid=(n // tile,),
        in_specs=[tok(D_RNN), tok(Q_W), tok(GATE_W), tok(D_MODEL), mod, mod, mod,
                  pl.BlockSpec((1, D_MODEL), lambda i: (0, 0)), full(wr), full(wa), full(wo)],
        out_specs=[tok(D_MODEL), pl.BlockSpec((D_MODEL, tile), lambda i: (0, i))],
        out_shape=[jax.ShapeDtypeStruct((n, D_MODEL), F32), jax.ShapeDtypeStruct((D_MODEL, n), BF16)],
        compiler_params=_params(("arbitrary",), 48),
    )(hg, o, gates, x2d, gt1, sh2, sc2, g2, wr, wa, wo)


def _vmax(a, b):
    if a is None:
        return b
    if b is None:
        return a
    return jnp.maximum(a, b)


def _vmin(a, b):
    if a is None or b is None:
        return None
    return jnp.minimum(a, b)


def _compare_exchange(xs, i, j):
    hi, lo = _vmax(xs[i], xs[j]), _vmin(xs[i], xs[j])
    xs[i], xs[j] = hi, lo


def _sort_pairs(n):
    pairs = []

    def merge(lo, hi, r):
        step = r * 2
        if step < hi - lo:
            merge(lo, hi, step)
            merge(lo + r, hi, step)
            for i in range(lo + r, hi - r, step):
                pairs.append((i, i + r))
        else:
            pairs.append((lo, lo + r))

    def sort(lo, hi):
        if hi - lo >= 1:
            mid = lo + (hi - lo) // 2
            sort(lo, mid)
            sort(mid + 1, hi)
            merge(lo, hi, 1)

    sort(0, n - 1)
    return pairs


_SORT16 = _sort_pairs(PEER_TOPK)


def _sort_desc(xs):
    xs = list(xs)
    for i, j in _SORT16:
        _compare_exchange(xs, i, j)
    return xs


def _bitonic_to_desc(xs):
    xs = list(xs)
    d = len(xs) // 2
    while d >= 1:
        for i in range(len(xs)):
            if (i & d) == 0:
                _compare_exchange(xs, i, i + d)
        d //= 2
    return xs


def _top_merge(xs, ys):
    k = len(xs)
    return _bitonic_to_desc([_vmax(xs[i], ys[k - 1 - i]) for i in range(k)])


def _top16_rows(s):
    rows = _sort_desc([s[r * SUBLANES:(r + 1) * SUBLANES, :] for r in range(N_KEYS // SUBLANES)])
    shift = SUBLANES // 2
    while shift >= 1:
        rows = _top_merge(rows, [pltpu.roll(x, shift, axis=0) for x in rows])
        shift //= 2
    return rows


def _top16_pair_sums(v1, v2):
    k = PEER_TOPK
    col = lambda j: [v1[i] + v2[j] for i in range(k // (j + 1))]
    g0 = col(0)
    row0_tail = [v1[0] + v2[j] for j in range(k // 2, k)]
    g1 = _bitonic_to_desc(col(1) + row0_tail[::-1])
    g2 = _sort_desc(col(2) + col(3) + col(4) + col(5) + col(6))
    g3 = col(7) + [None] * (k - 2)
    return _top_merge(_top_merge(g0, g1), _top_merge(g2, g3))


def _route_kernel(h2t_ref, wq_ref, keys_ref, s1_ref, c1_ref, s2_ref, e2_ref, tau_ref, qt_ref):
    qt_ref[...] = jnp.dot(wq_ref[...], h2t_ref[...], preferred_element_type=F32)

    def head(h, carry):
        base = pl.multiple_of(h * PEER_DK, PEER_DK)
        q1 = qt_ref[pl.ds(base, PEER_DHALF), :].astype(BF16)
        q2 = qt_ref[pl.ds(base + PEER_DHALF, PEER_DHALF), :].astype(BF16)
        s1 = jnp.dot(keys_ref[h, 0], q1, preferred_element_type=F32)
        s2 = jnp.dot(keys_ref[h, 1], q2, preferred_element_type=F32)
        v1 = _top16_rows(s1)
        v2 = _top16_rows(s2)
        top = _top16_pair_sums(v1, v2)
        z = jnp.ones_like(top[0])
        for f in top[1:]:
            z = z + jnp.exp(f - top[0])
        s1_ref[h] = s1
        s2_ref[h] = s2
        c1_ref[h] = jnp.exp(s1 - v1[0][0:1, :]) * (1.0 / z[0:1, :])
        e2_ref[h] = jnp.exp(s2 - v2[0][0:1, :])
        tau_ref[h] = top[PEER_TOPK - 1]
        return carry

    lax.fori_loop(0, PEER_HEADS, head, 0)


def _route(h2t, wq_t, keys_bf16, tile):
    n = h2t.shape[1]
    tab = pl.BlockSpec((PEER_HEADS, N_KEYS, tile), lambda i: (0, 0, i))
    tab_shape = jax.ShapeDtypeStruct((PEER_HEADS, N_KEYS, n), F32)
    return pl.pallas_call(
        _route_kernel,
        grid=(n // tile,),
        in_specs=[pl.BlockSpec((D_MODEL, tile), lambda i: (0, i)),
                  pl.BlockSpec(wq_t.shape, lambda i: (0, 0)),
                  pl.BlockSpec(keys_bf16.shape, lambda i: (0, 0, 0, 0))],
        out_specs=[tab, tab, tab, tab, pl.BlockSpec((PEER_HEADS, SUBLANES, tile), lambda i: (0, 0, i))],
        out_shape=[tab_shape, tab_shape, tab_shape, tab_shape,
                   jax.ShapeDtypeStruct((PEER_HEADS, SUBLANES, n), F32)],
        scratch_shapes=[pltpu.VMEM((PEER_HEADS * PEER_DK, tile), F32)],
        compiler_params=_params(("arbitrary",), 48),
    )(h2t, wq_t, keys_bf16)


def _peer_kernel(h2t_ref, u_ref, vt_ref, s1_ref, c1_ref, s2_ref, e2_ref, tau_ref, out_ref, act_ref, wgt_ref,
                 *, nblk):
    t = pl.program_id(0)
    tile = h2t_ref.shape[1]

    @pl.when(t == 0)
    def _():
        act_ref[...] = jnp.zeros_like(act_ref)
        wgt_ref[...] = jnp.zeros_like(wgt_ref)

    @pl.when(jnp.logical_or(t <= 2, (t + nblk - 2) % nblk == 0))
    def _():
        out_ref[...] = jnp.zeros_like(out_ref)

    out_ref[...] += jnp.dot(vt_ref[...], wgt_ref[...], preferred_element_type=F32)

    for a in range(PEER_A_PER_BLOCK):
        for r in range(N_KEYS // PEER_UNIT_ROWS):
            rows = pl.ds(r * PEER_UNIT_ROWS, PEER_UNIT_ROWS)
            erows = pl.ds(a * N_KEYS + r * PEER_UNIT_ROWS, PEER_UNIT_ROWS)
            gates = [jnp.zeros((PEER_UNIT_ROWS, LANES), F32) for _ in range(tile // LANES)]
            for h in range(PEER_HEADS):
                for g in range(tile // LANES):
                    sl = pl.ds(g * LANES, LANES)
                    pair = s2_ref[h, rows, sl] + s1_ref[h, a:a + 1, sl]
                    sel = jnp.where(pair >= tau_ref[h, 0:1, sl], e2_ref[h, rows, sl], 0.0)
                    gates[g] = gates[g] + sel * c1_ref[h, a:a + 1, sl]
            for g in range(tile // LANES):
                sl = pl.ds(g * LANES, LANES)
                wgt_ref[erows, sl] = (gates[g] * _gelu(act_ref[erows, sl])).astype(BF16)

    act_ref[...] = jnp.dot(u_ref[...], h2t_ref[...], preferred_element_type=F32)


def _peer(h2t, u_bf16, vt_bf16, s1, c1, s2, e2, tau, tile):
    n = h2t.shape[1]
    eb = PEER_EXPERT_BLOCK
    nblk = N_EXPERTS // eb
    ntile = n // tile
    tok = lambda f: jnp.clip(f // nblk, 0, ntile - 1)
    blk = lambda f: (f + 2 * nblk) % nblk
    tab = lambda rows: pl.BlockSpec((PEER_HEADS, rows, tile), lambda t: (0, 0, tok(t - 1)))
    ablk = pl.BlockSpec((PEER_HEADS, PEER_A_PER_BLOCK, tile), lambda t: (0, blk(t - 1), tok(t - 1)))
    return pl.pallas_call(
        functools.partial(_peer_kernel, nblk=nblk),
        grid=(ntile * nblk + 2,),
        in_specs=[pl.BlockSpec((D_MODEL, tile), lambda t: (0, tok(t))),
                  pl.BlockSpec((eb, D_MODEL), lambda t: (blk(t), 0)),
                  pl.BlockSpec((D_MODEL, eb), lambda t: (0, blk(t - 2))),
                  ablk, ablk, tab(N_KEYS), tab(N_KEYS), tab(SUBLANES)],
        out_specs=pl.BlockSpec((D_MODEL, tile), lambda t: (0, tok(t - 2))),
        out_shape=jax.ShapeDtypeStruct((D_MODEL, n), F32),
        scratch_shapes=[pltpu.VMEM((eb, tile), F32), pltpu.VMEM((eb, tile), BF16)],
        compiler_params=_params(("arbitrary",), 48),
    )(h2t, u_bf16, vt_bf16, s1, c1, s2, e2, tau)


def _final_kernel(x1_ref, pt_ref, g2_ref, gf_ref, y_ref):
    x2 = x1_ref[...] + g2_ref[0] * pt_ref[...].T
    y_ref[...] = _rmsnorm(x2, gf_ref[...])


def _final(x1, peer_t, gt2, g_final, tile, tiles_per_row):
    n = x1.shape[0]
    rmod = gt2.shape[1]
    return pl.pallas_call(
        _final_kernel,
        grid=(n // tile,),
        in_specs=[pl.BlockSpec((tile, D_MODEL), lambda i: (i, 0)),
                  pl.BlockSpec((D_MODEL, tile), lambda i: (0, i)),
                  pl.BlockSpec((1, rmod, D_MODEL), lambda i: (i // tiles_per_row, 0, 0)),
                  pl.BlockSpec((1, D_MODEL), lambda i: (0, 0))],
        out_specs=pl.BlockSpec((tile, D_MODEL), lambda i: (i, 0)),
        out_shape=jax.ShapeDtypeStruct((n, D_MODEL), F32),
        compiler_params=_params(("arbitrary",), 32),
    )(x1, peer_t, gt2, g_final)


def _rope_tables(pos):
    half = HEAD_DIM // 2
    inv = ROPE_THETA ** (-jnp.arange(half, dtype=F32) / half)
    ang = pos.astype(F32)[:, None] * inv[None, :]
    cos, sin = jnp.cos(ang), jnp.sin(ang)
    reps = LANES // HEAD_DIM
    return (jnp.tile(jnp.concatenate([cos, cos], axis=1), (1, reps)),
            jnp.tile(jnp.concatenate([-sin, sin], axis=1), (1, reps)))


def _block_diag(w):
    eye = jnp.eye(RNN_BLOCKS, dtype=w.dtype)
    return (eye[:, None, :, None] * w[:, :, None, :]).reshape(D_RNN, D_RNN)


def _channel_mixer(h2t, x1, gt2, g_final, wq_t, keys, u_bf16, vt_bf16, tile, peer_tile, tiles_per_row):
    s1, c1, s2, e2, tau = _route(h2t, wq_t, keys, tile)
    peer_t = _peer(h2t, u_bf16, vt_bf16, s1, c1, s2, e2, tau, peer_tile)
    return _final(x1, peer_t, gt2, g_final, tile, tiles_per_row)


def kernel(x_prompt, x_sample, state_conv, state_rnn, cache_win_k, cache_win_v, c_prompt, c_sample, g_norm1, g_norm2, g_final, w_mod, b_mod, w_in, conv_w, conv_b, w_a, b_a, w_i, b_i, lam, w_rnn_out, sinks, w_attn_out, w_out, w_pq, sub_keys, u_tab, v_tab):
    nb, seq, _ = x_prompt.shape
    ns = x_sample.shape[0]
    wbuf = cache_win_k.shape[2]
    assert x_sample.shape[1] == 1 and g_norm1.shape[0] == 1
    assert seq % TOKEN_TILE == 0 and seq % WINDOW == 0 and ns % LANES == 0 and (nb * seq) % PEER_TOKEN_TILE == 0
    l = 0

    w_in_b = w_in[l].astype(BF16)
    wa_d = _block_diag(w_a[l]).astype(BF16)
    wi_d = _block_diag(w_i[l]).astype(BF16)
    wr_b, wat_b, wo_b = w_rnn_out[l].astype(BF16), w_attn_out[l].astype(BF16), w_out[l].astype(BF16)
    wq_t = w_pq[l].T.astype(BF16)
    keys_b = sub_keys[l].astype(BF16)
    u_b = u_tab[l].astype(BF16)
    vt_b = v_tab[l].T.astype(BF16)
    row = lambda v: v.reshape(1, -1)
    g1, g2, gf = row(g_norm1[l]), row(g_norm2[l]), row(g_final)
    cb, ba, bi, lm = row(conv_b[l]), row(b_a[l]), row(b_i[l]), row(lam[l])

    mod = _modulation(jnp.concatenate([c_prompt, c_sample], axis=0), w_mod[l], b_mod[l])
    mod_p = [m.reshape(nb, 1, D_MODEL) for m in jnp.split(mod[:nb], 6, axis=-1)]
    mod_s = [m.reshape(1, ns, D_MODEL) for m in jnp.split(mod[nb:], 6, axis=-1)]

    tile = TOKEN_TILE
    tpr = seq // tile
    xp = x_prompt.reshape(nb * seq, D_MODEL)
    cos_p, sin_p = _rope_tables(jnp.arange(seq))
    xr, gr, q, k, v, gates = _inproj(xp, mod_p[0], mod_p[1], g1, cos_p, sin_p, w_in_b, tile, tpr, tpr)
    hg, rnn_p = _rnn_prompt(xr, gr, conv_w[l], cb, wa_d, ba, wi_d, bi, lm, nb, seq, tile)
    o = _attn_prompt(sinks[l], q, k, v, nb, seq)
    x1, h2t = _merge(hg, o, gates, xp, mod_p[2], mod_p[3], mod_p[4], g2, wr_b, wat_b, wo_b, tile, tpr)
    y_p = _channel_mixer(h2t, x1, mod_p[5], gf, wq_t, keys_b, u_b, vt_b, tile, PEER_TOKEN_TILE, tpr)
    y_prompt = y_p.reshape(nb, seq, D_MODEL)
    conv_prompt = xr.reshape(nb, seq, D_RNN)[:, seq - (CONV_W - 1):][None]
    rnn_prompt = rnn_p.reshape(1, nb, D_RNN)
    nbuf = min(WINDOW, seq)
    win_k_prompt = k.reshape(nb, seq, N_KV, HEAD_DIM)[:, seq - nbuf:][None]
    win_v_prompt = v.reshape(nb, seq, N_KV, HEAD_DIM)[:, seq - nbuf:][None]

    xs = x_sample.reshape(ns, D_MODEL)
    cos_s, sin_s = _rope_tables(jnp.full((ns,), PAST_LEN))
    xr_s, gr_s, q_s, k_s, v_s, gates_s = _inproj(xs, mod_s[0], mod_s[1], g1, cos_s, sin_s, w_in_b, ns, 1, 1)
    sc = state_conv[l]
    hg_s, h_s = _rnn_sample(xr_s, gr_s, sc[:, 0], sc[:, 1], sc[:, 2], state_rnn[l],
                            conv_w[l], cb, wa_d, ba, wi_d, bi, lm)
    head_group = (jnp.arange(N_HEADS)[:, None] // GROUP == jnp.arange(N_KV)[None, :])
    qx = jnp.where(head_group[None, :, :, None], q_s.reshape(ns, N_HEADS, 1, HEAD_DIM), 0)
    qx = qx.reshape(ns, N_HEADS, KV_W).astype(BF16)
    ck = cache_win_k[l].reshape(ns, wbuf, KV_W)
    cv = cache_win_v[l].reshape(ns, wbuf, KV_W)
    ox = _attn_sample(sinks[l].reshape(N_HEADS, 1), qx, ck, cv, k_s.reshape(ns, 1, KV_W), v_s.reshape(ns, 1, KV_W))
    o_s = jnp.where(head_group[None, :, :, None], ox.reshape(ns, N_HEADS, N_KV, HEAD_DIM), 0.0).sum(axis=2)
    o_s = o_s.reshape(ns, Q_W).astype(BF16)
    x1_s, h2t_s = _merge(hg_s, o_s, gates_s, xs, mod_s[2], mod_s[3], mod_s[4], g2, wr_b, wat_b, wo_b, ns, 1)
    y_s = _channel_mixer(h2t_s, x1_s, mod_s[5], gf, wq_t, keys_b, u_b, vt_b, ns, ns, 1)
    y_sample = y_s.reshape(ns, 1, D_MODEL)
    conv_sample = jnp.concatenate([sc[:, 1:], xr_s[:, None, :]], axis=1)[None]
    rnn_sample = h_s[None]
    win_k_sample = jnp.concatenate([ck[:, 1:], k_s[:, None, :]], axis=1).reshape(1, ns, wbuf, N_KV, HEAD_DIM)
    win_v_sample = jnp.concatenate([cv[:, 1:], v_s[:, None, :]], axis=1).reshape(1, ns, wbuf, N_KV, HEAD_DIM)

    return (y_prompt, y_sample, conv_prompt, rnn_prompt, win_k_prompt, win_v_prompt,
            conv_sample, rnn_sample, win_k_sample, win_v_sample)
```

```python
import functools

import jax
import jax.numpy as jnp
from jax import lax
from jax.experimental import pallas as pl
from jax.experimental.pallas import tpu as pltpu

F32 = jnp.float32
BF16 = jnp.bfloat16

D_MODEL = 1024
PAST_LEN = 8192
D_RNN = 1280
RNN_BLOCKS = 16
RNN_BW = D_RNN // RNN_BLOCKS
CONV_W = 4
LRU_C = 8.0
N_HEADS = 16
N_KV = 4
HEAD_DIM = 64
GROUP = N_HEADS // N_KV
WINDOW = 128
ROPE_THETA = 10000.0
Q_W = N_HEADS * HEAD_DIM
KV_W = N_KV * HEAD_DIM
PEER_HEADS = 8
N_KEYS = 128
N_EXPERTS = N_KEYS * N_KEYS
PEER_DK = 256
PEER_DHALF = PEER_DK // 2
PEER_TOPK = 16
EPS = 1e-6
NEG = -1e30
GATE_W = 2 * D_MODEL

LANES = 128
SUBLANES = 8
MIB = 1024 * 1024

TOKEN_TILE = 256
PEER_TOKEN_TILE = 512
PEER_UNIT_ROWS = 32
PEER_EXPERT_BLOCK = 2048
PEER_A_PER_BLOCK = PEER_EXPERT_BLOCK // N_KEYS


def _params(semantics, vmem_mib):
    return pltpu.CompilerParams(dimension_semantics=semantics, vmem_limit_bytes=vmem_mib * MIB)


def _gelu(x):
    return 0.5 * x * (1.0 + lax.erf(x * (2.0 ** -0.5)))


def _rmsnorm(x, g):
    return x * lax.rsqrt(jnp.mean(x * x, axis=-1, keepdims=True) + EPS) * g


def _mod_kernel(c_ref, w_ref, b_ref, o_ref):
    c = c_ref[...]
    s = (c * jax.nn.sigmoid(c)).astype(BF16)
    o_ref[...] = jnp.dot(s, w_ref[...].astype(BF16), preferred_element_type=F32) + b_ref[...]


def _modulation(c_all, w_mod, b_mod):
    rows = c_all.shape[0]
    nblk = w_mod.shape[1] // D_MODEL
    return pl.pallas_call(
        _mod_kernel,
        grid=(nblk,),
        in_specs=[pl.BlockSpec((rows, D_MODEL), lambda j: (0, 0)),
                  pl.BlockSpec((D_MODEL, D_MODEL), lambda j: (0, j)),
                  pl.BlockSpec((1, D_MODEL), lambda j: (0, j))],
        out_specs=pl.BlockSpec((rows, D_MODEL), lambda j: (0, j)),
        out_shape=jax.ShapeDtypeStruct((rows, w_mod.shape[1]), F32),
        compiler_params=_params(("arbitrary",), 32),
    )(c_all, w_mod, b_mod.reshape(1, -1))


def _rope(x, cos, sin_signed):
    width = x.shape[1]
    reps = width // LANES
    cosf = jnp.concatenate([cos] * reps, axis=1)
    sinf = jnp.concatenate([sin_signed] * reps, axis=1)
    half = HEAD_DIM // 2
    upper = pltpu.roll(x, width - half, axis=1)
    lower = pltpu.roll(x, half, axis=1)
    lane = lax.broadcasted_iota(jnp.int32, x.shape, 1)
    rot = jnp.where((lane & half) == 0, upper, lower)
    return x * cosf + rot * sinf


def _inproj_kernel(x_ref, sh_ref, sc_ref, g_ref, cos_ref, sin_ref, w_ref,
                   xr_ref, gr_ref, q_ref, k_ref, v_ref, gt_ref):
    x = x_ref[...]
    hn = (_rmsnorm(x, g_ref[...]) * (1.0 + sc_ref[0]) + sh_ref[0]).astype(BF16)

    def proj(c0, width):
        return jnp.dot(hn, w_ref[:, c0:c0 + width], preferred_element_type=F32)

    c = 0
    xr_ref[...] = proj(c, D_RNN); c += D_RNN
    gr_ref[...] = proj(c, D_RNN); c += D_RNN
    q_ref[...] = _rope(proj(c, Q_W), cos_ref[...], sin_ref[...]).astype(BF16); c += Q_W
    k_ref[...] = _rope(proj(c, KV_W), cos_ref[...], sin_ref[...]); c += KV_W
    v_ref[...] = proj(c, KV_W); c += KV_W
    gt_ref[...] = proj(c, GATE_W)


def _inproj(x2d, sh, sc, g, cos, sin_signed, w_in_bf16, tile, tiles_per_row, rope_tiles):
    n = x2d.shape[0]
    rmod = sh.shape[1]
    d_in = w_in_bf16.shape[1]
    tok = lambda width: pl.BlockSpec((tile, width), lambda i: (i, 0))
    mod = pl.BlockSpec((1, rmod, D_MODEL), lambda i: (i // tiles_per_row, 0, 0))
    rope = pl.BlockSpec((tile, LANES), lambda i: (i % rope_tiles, 0))
    return pl.pallas_call(
        _inproj_kernel,
        grid=(n // tile,),
        in_specs=[tok(D_MODEL), mod, mod, pl.BlockSpec((1, D_MODEL), lambda i: (0, 0)), rope, rope,
                  pl.BlockSpec((D_MODEL, d_in), lambda i: (0, 0), pipeline_mode=pl.Buffered(1))],
        out_specs=[tok(D_RNN), tok(D_RNN), tok(Q_W), tok(KV_W), tok(KV_W), tok(GATE_W)],
        out_shape=[jax.ShapeDtypeStruct((n, D_RNN), F32), jax.ShapeDtypeStruct((n, D_RNN), F32),
                   jax.ShapeDtypeStruct((n, Q_W), BF16), jax.ShapeDtypeStruct((n, KV_W), F32),
                   jax.ShapeDtypeStruct((n, KV_W), F32), jax.ShapeDtypeStruct((n, GATE_W), F32)],
        compiler_params=_params(("arbitrary",), 48),
    )(x2d, sh, sc, g, cos, sin_signed, w_in_bf16)


def _log_sigmoid(x):
    return -(jnp.maximum(-x, 0.0) + jnp.log1p(jnp.exp(-jnp.abs(x))))


def _lru_coeffs(xc, wa_ref, ba_ref, wi_ref, bi_ref, lam_ref):
    xb = xc.astype(BF16)
    r = jax.nn.sigmoid(jnp.dot(xb, wa_ref[...], preferred_element_type=F32) + ba_ref[...])
    i = jax.nn.sigmoid(jnp.dot(xb, wi_ref[...], preferred_element_type=F32) + bi_ref[...])
    log_a = LRU_C * r * _log_sigmoid(lam_ref[...])
    a = jnp.exp(log_a)
    mult = jnp.sqrt(-jnp.tanh(log_a) * (1.0 + a * a))
    return a, mult * (i * xc)


def _shift_rows(x, s, fill):
    rows = x.shape[0]
    if s % SUBLANES == 0:
        return jnp.concatenate([jnp.full((s, x.shape[1]), fill, x.dtype), x[:rows - s]], axis=0)
    rolled = pltpu.roll(x, s, axis=0)
    row = lax.broadcasted_iota(jnp.int32, x.shape, 0)
    return jnp.where(row < s, fill, rolled)


def _rnn_prompt_kernel(xr_ref, gr_ref, cw_ref, cb_ref, wa_ref, ba_ref, wi_ref, bi_ref, lam_ref,
                       hg_ref, hlast_ref, tail_ref, hc_ref):
    t = pl.program_id(1)
    tile = xr_ref.shape[0]

    @pl.when(t == 0)
    def _():
        tail_ref[...] = jnp.zeros_like(tail_ref)
        hc_ref[...] = jnp.zeros_like(hc_ref)

    xr = xr_ref[...]
    cat = jnp.concatenate([tail_ref[...], xr], axis=0)
    xc = cb_ref[...] + cw_ref[CONV_W - 1:CONV_W, :] * xr
    for s in range(1, CONV_W):
        shifted = pltpu.roll(cat, s, axis=0)[SUBLANES:, :]
        xc = xc + cw_ref[CONV_W - 1 - s:CONV_W - s, :] * shifted
    tail_ref[...] = xr[tile - SUBLANES:, :]

    a, b = _lru_coeffs(xc, wa_ref, ba_ref, wi_ref, bi_ref, lam_ref)
    s = 1
    while s < tile:
        a_prev = _shift_rows(a, s, 1.0)
        b_prev = _shift_rows(b, s, 0.0)
        b = a * b_prev + b
        a = a * a_prev
        s *= 2
    h = a * hc_ref[0:1, :] + b
    hlast = h[tile - 1:tile, :]
    hc_ref[...] = jnp.broadcast_to(hlast, hc_ref.shape)
    hlast_ref[0] = hlast
    hg_ref[...] = (h * _gelu(gr_ref[...])).astype(BF16)


def _rnn_prompt(xr, gr, conv_w, conv_b, wa, ba, wi, bi, lam, nb, seq, tile):
    n = xr.shape[0]
    tpr = seq // tile
    tok = pl.BlockSpec((tile, D_RNN), lambda b, t: (b * tpr + t, 0))
    row = pl.BlockSpec((1, D_RNN), lambda b, t: (0, 0))
    sq = pl.BlockSpec((D_RNN, D_RNN), lambda b, t: (0, 0))
    return pl.pallas_call(
        _rnn_prompt_kernel,
        grid=(nb, tpr),
        in_specs=[tok, tok, pl.BlockSpec((CONV_W, D_RNN), lambda b, t: (0, 0)), row, sq, row, sq, row, row],
        out_specs=[tok, pl.BlockSpec((1, 1, D_RNN), lambda b, t: (b, 0, 0))],
        out_shape=[jax.ShapeDtypeStruct((n, D_RNN), BF16), jax.ShapeDtypeStruct((nb, 1, D_RNN), F32)],
        scratch_shapes=[pltpu.VMEM((SUBLANES, D_RNN), F32), pltpu.VMEM((SUBLANES, D_RNN), F32)],
        compiler_params=_params(("arbitrary", "arbitrary"), 48),
    )(xr, gr, conv_w, conv_b, wa, ba, wi, bi, lam)


def _rnn_sample_kernel(xr_ref, gr_ref, c0_ref, c1_ref, c2_ref, h0_ref, cw_ref, cb_ref,
                       wa_ref, ba_ref, wi_ref, bi_ref, lam_ref, hg_ref, h_ref):
    xc = (cb_ref[...] + cw_ref[0:1, :] * c0_ref[...] + cw_ref[1:2, :] * c1_ref[...]
          + cw_ref[2:3, :] * c2_ref[...] + cw_ref[3:4, :] * xr_ref[...])
    a, b = _lru_coeffs(xc, wa_ref, ba_ref, wi_ref, bi_ref, lam_ref)
    h = b + a * h0_ref[...]
    h_ref[...] = h
    hg_ref[...] = (h * _gelu(gr_ref[...])).astype(BF16)


def _rnn_sample(xr, gr, c0, c1, c2, h0, conv_w, conv_b, wa, ba, wi, bi, lam):
    n = xr.shape[0]
    return pl.pallas_call(
        _rnn_sample_kernel,
        out_shape=[jax.ShapeDtypeStruct((n, D_RNN), BF16), jax.ShapeDtypeStruct((n, D_RNN), F32)],
        compiler_params=pltpu.CompilerParams(vmem_limit_bytes=48 * MIB),
    )(xr, gr, c0, c1, c2, h0, conv_w, conv_b, wa, ba, wi, bi, lam)


def _attn_prompt_kernel(sink_ref, q_ref, kp_ref, kc_ref, vp_ref, vc_ref, o_ref):
    j = pl.program_id(1)
    kk = jnp.concatenate([kp_ref[...], kc_ref[...]], axis=0).astype(BF16)
    vv = jnp.concatenate([vp_ref[...], vc_ref[...]], axis=0).astype(BF16)
    qi = lax.broadcasted_iota(jnp.int32, (WINDOW, 2 * WINDOW), 0)
    ci = lax.broadcasted_iota(jnp.int32, (WINDOW, 2 * WINDOW), 1)
    dist = qi + WINDOW - ci
    kpos = ci + (j - 1) * WINDOW
    mask = (dist >= 0) & (dist <= WINDOW) & (kpos >= 0)
    q = q_ref[...]
    outs = []
    for h in range(N_HEADS):
        g = h // GROUP
        qh = q[:, h * HEAD_DIM:(h + 1) * HEAD_DIM]
        kh = kk[:, g * HEAD_DIM:(g + 1) * HEAD_DIM]
        vh = vv[:, g * HEAD_DIM:(g + 1) * HEAD_DIM]
        s = lax.dot_general(qh, kh, (((1,), (1,)), ((), ())), preferred_element_type=F32) * (HEAD_DIM ** -0.5)
        s = jnp.where(mask, s, NEG)
        sink = sink_ref[h]
        m = jnp.maximum(jnp.max(s, axis=-1, keepdims=True), sink)
        e = jnp.exp(s - m)
        p = e / (jnp.sum(e, axis=-1, keepdims=True) + jnp.exp(sink - m))
        outs.append(jnp.dot(p.astype(BF16), vh, preferred_element_type=F32))
    o_ref[...] = jnp.concatenate(outs, axis=1).astype(BF16)


def _attn_prompt(sinks, q, k, v, nb, seq):
    n = q.shape[0]
    nblk = seq // WINDOW
    cur = lambda b, j: (b * nblk + j, 0)
    prev = lambda b, j: (b * nblk + jnp.maximum(j - 1, 0), 0)
    return pl.pallas_call(
        _attn_prompt_kernel,
        grid=(nb, nblk),
        in_specs=[pl.BlockSpec(memory_space=pltpu.SMEM),
                  pl.BlockSpec((WINDOW, Q_W), cur),
                  pl.BlockSpec((WINDOW, KV_W), prev), pl.BlockSpec((WINDOW, KV_W), cur),
                  pl.BlockSpec((WINDOW, KV_W), prev), pl.BlockSpec((WINDOW, KV_W), cur)],
        out_specs=pl.BlockSpec((WINDOW, Q_W), cur),
        out_shape=jax.ShapeDtypeStruct((n, Q_W), BF16),
        compiler_params=_params(("arbitrary", "arbitrary"), 32),
    )(sinks, q, k, k, v, v)


SAMPLE_ATTN_ROWS = 8


def _attn_sample_kernel(sink_ref, qx_ref, ck_ref, cv_ref, kn_ref, vn_ref, o_ref, wk_ref, wv_ref):
    sink = sink_ref[...]
    wbuf = ck_ref.shape[1]
    last = lax.broadcasted_iota(jnp.int32, (wbuf, KV_W), 0) == wbuf - 1
    for r in range(SAMPLE_ATTN_ROWS):
        qx = qx_ref[r]
        wk_ref[r] = jnp.where(last, kn_ref[r], pltpu.roll(ck_ref[r], wbuf - 1, axis=0))
        wv_ref[r] = jnp.where(last, vn_ref[r], pltpu.roll(cv_ref[r], wbuf - 1, axis=0))
        kb = ck_ref[r].astype(BF16)
        vb = cv_ref[r].astype(BF16)
        kn = kn_ref[r].astype(BF16).astype(F32)
        vn = vn_ref[r].astype(BF16).astype(F32)
        scale = HEAD_DIM ** -0.5
        s = lax.dot_general(qx, kb, (((1,), (1,)), ((), ())), preferred_element_type=F32) * scale
        s_new = jnp.sum(qx.astype(F32) * kn, axis=-1, keepdims=True) * scale
        m = jnp.maximum(jnp.maximum(jnp.max(s, axis=-1, keepdims=True), s_new), sink)
        e = jnp.exp(s - m)
        e_new = jnp.exp(s_new - m)
        den = jnp.sum(e, axis=-1, keepdims=True) + e_new + jnp.exp(sink - m)
        p = (e / den).astype(BF16)
        p_new = (e_new / den).astype(BF16).astype(F32)
        o_ref[r] = jnp.dot(p, vb, preferred_element_type=F32) + p_new * vn


def _attn_sample(sinks_col, qx, cache_k, cache_v, k_new, v_new):
    nb, wbuf = cache_k.shape[0], cache_k.shape[1]
    rows = SAMPLE_ATTN_ROWS
    blk = lambda d1, d2: pl.BlockSpec((rows, d1, d2), lambda i: (i, 0, 0))
    return pl.pallas_call(
        _attn_sample_kernel,
        grid=(nb // rows,),
        in_specs=[pl.BlockSpec((N_HEADS, 1), lambda i: (0, 0)), blk(N_HEADS, KV_W),
                  blk(wbuf, KV_W), blk(wbuf, KV_W), blk(1, KV_W), blk(1, KV_W)],
        out_specs=[blk(N_HEADS, KV_W), blk(wbuf, KV_W), blk(wbuf, KV_W)],
        out_shape=[jax.ShapeDtypeStruct((nb, N_HEADS, KV_W), F32),
                   jax.ShapeDtypeStruct((nb, wbuf, KV_W), F32), jax.ShapeDtypeStruct((nb, wbuf, KV_W), F32)],
        compiler_params=_params(("arbitrary",), 32),
    )(sinks_col, qx, cache_k, cache_v, k_new, v_new)


def _merge_kernel(hg_ref, o_ref, gt_ref, x_ref, g1_ref, sh_ref, sc_ref, gn_ref, wr_ref, wa_ref, wo_ref,
                  x1_ref, h2t_ref):
    y_r = jnp.dot(hg_ref[...], wr_ref[...], preferred_element_type=F32)
    y_a = jnp.dot(o_ref[...], wa_ref[...], preferred_element_type=F32)
    merged = (jax.nn.sigmoid(gt_ref[:, :D_MODEL]) * y_r + jax.nn.sigmoid(gt_ref[:, D_MODEL:]) * y_a)
    x1 = x_ref[...] + g1_ref[0] * jnp.dot(merged.astype(BF16), wo_ref[...], preferred_element_type=F32)
    x1_ref[...] = x1
    h2 = _rmsnorm(x1, gn_ref[...]) * (1.0 + sc_ref[0]) + sh_ref[0]
    h2t_ref[...] = h2.T.astype(BF16)


def _merge(hg, o, gates, x2d, gt1, sh2, sc2, g2, wr, wa, wo, tile, tiles_per_row):
    n = x2d.shape[0]
    rmod = gt1.shape[1]
    tok = lambda width: pl.BlockSpec((tile, width), lambda i: (i, 0))
    mod = pl.BlockSpec((1, rmod, D_MODEL), lambda i: (i // tiles_per_row, 0, 0))
    full = lambda a: pl.BlockSpec(a.shape, lambda i: (0, 0))
    return pl.pallas_call(
        _merge_kernel,
        grid=(n // tile,),
        in_specs=[tok(D_RNN), tok(Q_W), tok(GATE_W), tok(D_MODEL), mod, mod, mod,
                  pl.BlockSpec((1, D_MODEL), lambda i: (0, 0)), full(wr), full(wa), full(wo)],
        out_specs=[tok(D_MODEL), pl.BlockSpec((D_MODEL, tile), lambda i: (0, i))],
        out_shape=[jax.ShapeDtypeStruct((n, D_MODEL), F32), jax.ShapeDtypeStruct((D_MODEL, n), BF16)],
        compiler_params=_params(("arbitrary",), 48),
    )(hg, o, gates, x2d, gt1, sh2, sc2, g2, wr, wa, wo)


def _vmax(a, b):
    if a is None:
        return b
    if b is None:
        return a
    return jnp.maximum(a, b)


def _vmin(a, b):
    if a is None or b is None:
        return None
    return jnp.minimum(a, b)


def _compare_exchange(xs, i, j):
    hi, lo = _vmax(xs[i], xs[j]), _vmin(xs[i], xs[j])
    xs[i], xs[j] = hi, lo


def _sort_pairs(n):
    pairs = []

    def merge(lo, hi, r):
        step = r * 2
        if step < hi - lo:
            merge(lo, hi, step)
            merge(lo + r, hi, step)
            for i in range(lo + r, hi - r, step):
                pairs.append((i, i + r))
        else:
            pairs.append((lo, lo + r))

    def sort(lo, hi):
        if hi - lo >= 1:
            mid = lo + (hi - lo) // 2
            sort(lo, mid)
            sort(mid + 1, hi)
            merge(lo, hi, 1)

    sort(0, n - 1)
    return pairs


_SORT16 = _sort_pairs(PEER_TOPK)


def _sort_desc(xs):
    xs = list(xs)
    for i, j in _SORT16:
        _compare_exchange(xs, i, j)
    return xs


def _bitonic_to_desc(xs):
    xs = list(xs)
    d = len(xs) // 2
    while d >= 1:
        for i in range(len(xs)):
            if (i & d) == 0:
                _compare_exchange(xs, i, i + d)
        d //= 2
    return xs


def _top_merge(xs, ys):
    k = len(xs)
    return _bitonic_to_desc([_vmax(xs[i], ys[k - 1 - i]) for i in range(k)])


def _top16_rows(s):
    rows = _sort_desc([s[r * SUBLANES:(r + 1) * SUBLANES, :] for r in range(N_KEYS // SUBLANES)])
    shift = SUBLANES // 2
    while shift >= 1:
        rows = _top_merge(rows, [pltpu.roll(x, shift, axis=0) for x in rows])
        shift //= 2
    return rows


def _top16_pair_sums(v1, v2):
    k = PEER_TOPK
    col = lambda j: [v1[i] + v2[j] for i in range(k // (j + 1))]
    g0 = col(0)
    row0_tail = [v1[0] + v2[j] for j in range(k // 2, k)]
    g1 = _bitonic_to_desc(col(1) + row0_tail[::-1])
    g2 = _sort_desc(col(2) + col(3) + col(4) + col(5) + col(6))
    g3 = col(7) + [None] * (k - 2)
    return _top_merge(_top_merge(g0, g1), _top_merge(g2, g3))


PEER_SPECIAL_RANKS = 4


def _threshold_chain(s1_tile, v2_head, tau):
    thr = jnp.full(s1_tile.shape, jnp.inf, F32)
    for v in v2_head:
        thr = jnp.where(s1_tile + v >= tau, v, thr)
    return thr


def _second_key_thresholds(s1, v1, v2, tau):
    k = PEER_TOPK
    by_rank = [_threshold_chain(v1[i], v2[:k // (i + 1)], tau) for i in range(PEER_SPECIAL_RANKS)]
    tiles = []
    for r in range(N_KEYS // SUBLANES):
        tile = s1[r * SUBLANES:(r + 1) * SUBLANES, :]
        thr = _threshold_chain(tile, v2[:k // (PEER_SPECIAL_RANKS + 1)], tau)
        for i in reversed(range(PEER_SPECIAL_RANKS)):
            thr = jnp.where(tile >= v1[i], by_rank[i], thr)
        tiles.append(thr)
    return tiles


def _route_kernel(h2t_ref, wq_ref, keys_ref, thr_ref, c1_ref, s2_ref, e2_ref, qt_ref):
    qt_ref[...] = jnp.dot(wq_ref[...], h2t_ref[...], preferred_element_type=F32)

    def head(h, carry):
        base = pl.multiple_of(h * PEER_DK, PEER_DK)
        q1 = qt_ref[pl.ds(base, PEER_DHALF), :].astype(BF16)
        q2 = qt_ref[pl.ds(base + PEER_DHALF, PEER_DHALF), :].astype(BF16)
        s1 = jnp.dot(keys_ref[h, 0], q1, preferred_element_type=F32)
        s2 = jnp.dot(keys_ref[h, 1], q2, preferred_element_type=F32)
        v1 = _top16_rows(s1)
        v2 = _top16_rows(s2)
        top = _top16_pair_sums(v1, v2)
        tau = top[PEER_TOPK - 1]
        z = jnp.ones_like(top[0])
        for f in top[1:]:
            z = z + jnp.exp(f - top[0])
        for r, thr in enumerate(_second_key_thresholds(s1, v1, v2, tau)):
            thr_ref[h, pl.ds(r * SUBLANES, SUBLANES), :] = thr
        s2_ref[h] = s2
        c1_ref[h] = jnp.exp(s1 - v1[0][0:1, :]) * (0.5 / z[0:1, :])
        e2_ref[h] = jnp.exp(s2 - v2[0][0:1, :])
        return carry

    lax.fori_loop(0, PEER_HEADS, head, 0)


def _route(h2t, wq_t, keys_bf16, tile):
    n = h2t.shape[1]
    tab = pl.BlockSpec((PEER_HEADS, N_KEYS, tile), lambda i: (0, 0, i))
    tab_shape = jax.ShapeDtypeStruct((PEER_HEADS, N_KEYS, n), F32)
    return pl.pallas_call(
        _route_kernel,
        grid=(n // tile,),
        in_specs=[pl.BlockSpec((D_MODEL, tile), lambda i: (0, i)),
                  pl.BlockSpec(wq_t.shape, lambda i: (0, 0)),
                  pl.BlockSpec(keys_bf16.shape, lambda i: (0, 0, 0, 0))],
        out_specs=[tab, tab, tab, tab],
        out_shape=[tab_shape, tab_shape, tab_shape, tab_shape],
        scratch_shapes=[pltpu.VMEM((PEER_HEADS * PEER_DK, tile), F32)],
        compiler_params=_params(("arbitrary",), 48),
    )(h2t, wq_t, keys_bf16)


def _peer_kernel(h2t_ref, u_ref, vt_ref, thr_ref, c1_ref, s2_ref, e2_ref, out_ref, act_ref, wgt_ref):
    j = pl.program_id(1)
    tile = h2t_ref.shape[1]

    @pl.when(j == 0)
    def _():
        out_ref[...] = jnp.zeros_like(out_ref)

    act_ref[...] = jnp.dot(u_ref[...], h2t_ref[...], preferred_element_type=F32)

    def row_group(r, carry):
        r0 = pl.multiple_of(r * PEER_UNIT_ROWS, PEER_UNIT_ROWS)
        rows = pl.ds(r0, PEER_UNIT_ROWS)
        for a in range(PEER_A_PER_BLOCK):
            erows = pl.ds(a * N_KEYS + r0, PEER_UNIT_ROWS)
            gates = [jnp.zeros((PEER_UNIT_ROWS, LANES), F32) for _ in range(tile // LANES)]
            for h in range(PEER_HEADS):
                for g in range(tile // LANES):
                    sl = pl.ds(g * LANES, LANES)
                    sel = jnp.where(s2_ref[h, rows, sl] >= thr_ref[h, a:a + 1, sl], e2_ref[h, rows, sl], 0.0)
                    gates[g] = gates[g] + sel * c1_ref[h, a:a + 1, sl]
            for g in range(tile // LANES):
                sl = pl.ds(g * LANES, LANES)
                x = act_ref[erows, sl]
                wgt_ref[erows, sl] = (gates[g] * (x * (1.0 + lax.erf(x * (2.0 ** -0.5))))).astype(BF16)
        return carry

    lax.fori_loop(0, N_KEYS // PEER_UNIT_ROWS, row_group, 0)
    out_ref[...] += jnp.dot(vt_ref[...], wgt_ref[...], preferred_element_type=F32)


def _peer(h2t, u_bf16, vt_bf16, thr, c1, s2, e2, tile):
    n = h2t.shape[1]
    eb = PEER_EXPERT_BLOCK
    tab = lambda rows: pl.BlockSpec((PEER_HEADS, rows, tile), lambda i, j: (0, 0, i))
    ablk = pl.BlockSpec((PEER_HEADS, PEER_A_PER_BLOCK, tile), lambda i, j: (0, j, i))
    return pl.pallas_call(
        _peer_kernel,
        grid=(n // tile, N_EXPERTS // eb),
        in_specs=[pl.BlockSpec((D_MODEL, tile), lambda i, j: (0, i)),
                  pl.BlockSpec((eb, D_MODEL), lambda i, j: (j, 0)),
                  pl.BlockSpec((D_MODEL, eb), lambda i, j: (0, j)),
                  ablk, ablk, tab(N_KEYS), tab(N_KEYS)],
        out_specs=pl.BlockSpec((D_MODEL, tile), lambda i, j: (0, i)),
        out_shape=jax.ShapeDtypeStruct((D_MODEL, n), F32),
        scratch_shapes=[pltpu.VMEM((eb, tile), F32), pltpu.VMEM((eb, tile), BF16)],
        compiler_params=_params(("arbitrary", "arbitrary"), 48),
    )(h2t, u_bf16, vt_bf16, thr, c1, s2, e2)


def _final_kernel(x1_ref, pt_ref, g2_ref, gf_ref, y_ref):
    x2 = x1_ref[...] + g2_ref[0] * pt_ref[...].T
    y_ref[...] = _rmsnorm(x2, gf_ref[...])


def _final(x1, peer_t, gt2, g_final, tile, tiles_per_row):
    n = x1.shape[0]
    rmod = gt2.shape[1]
    return pl.pallas_call(
        _final_kernel,
        grid=(n // tile,),
        in_specs=[pl.BlockSpec((tile, D_MODEL), lambda i: (i, 0)),
                  pl.BlockSpec((D_MODEL, tile), lambda i: (0, i)),
                  pl.BlockSpec((1, rmod, D_MODEL), lambda i: (i // tiles_per_row, 0, 0)),
                  pl.BlockSpec((1, D_MODEL), lambda i: (0, 0))],
        out_specs=pl.BlockSpec((tile, D_MODEL), lambda i: (i, 0)),
        out_shape=jax.ShapeDtypeStruct((n, D_MODEL), F32),
        compiler_params=_params(("arbitrary",), 32),
    )(x1, peer_t, gt2, g_final)


def _rope_tables(pos):
    half = HEAD_DIM // 2
    inv = ROPE_THETA ** (-jnp.arange(half, dtype=F32) / half)
    ang = pos.astype(F32)[:, None] * inv[None, :]
    cos, sin = jnp.cos(ang), jnp.sin(ang)
    reps = LANES // HEAD_DIM
    return (jnp.tile(jnp.concatenate([cos, cos], axis=1), (1, reps)),
            jnp.tile(jnp.concatenate([-sin, sin], axis=1), (1, reps)))


def _block_diag(w):
    eye = jnp.eye(RNN_BLOCKS, dtype=w.dtype)
    return (eye[:, None, :, None] * w[:, :, None, :]).reshape(D_RNN, D_RNN)


def _channel_mixer(h2t, x1, gt2, g_final, wq_t, keys, u_bf16, vt_bf16, tile, peer_tile, tiles_per_row):
    thr, c1, s2, e2 = _route(h2t, wq_t, keys, tile)
    peer_t = _peer(h2t, u_bf16, vt_bf16, thr, c1, s2, e2, peer_tile)
    return _final(x1, peer_t, gt2, g_final, tile, tiles_per_row)


def kernel(x_prompt, x_sample, state_conv, state_rnn, cache_win_k, cache_win_v, c_prompt, c_sample, g_norm1, g_norm2, g_final, w_mod, b_mod, w_in, conv_w, conv_b, w_a, b_a, w_i, b_i, lam, w_rnn_out, sinks, w_attn_out, w_out, w_pq, sub_keys, u_tab, v_tab):
    nb, seq, _ = x_prompt.shape
    ns = x_sample.shape[0]
    wbuf = cache_win_k.shape[2]
    assert x_sample.shape[1] == 1 and g_norm1.shape[0] == 1
    assert seq % TOKEN_TILE == 0 and seq % WINDOW == 0 and ns % LANES == 0 and (nb * seq) % PEER_TOKEN_TILE == 0
    l = 0

    w_in_b = w_in[l].astype(BF16)
    wa_d = _block_diag(w_a[l]).astype(BF16)
    wi_d = _block_diag(w_i[l]).astype(BF16)
    wr_b, wat_b, wo_b = w_rnn_out[l].astype(BF16), w_attn_out[l].astype(BF16), w_out[l].astype(BF16)
    wq_t = w_pq[l].T.astype(BF16)
    keys_b = sub_keys[l].astype(BF16)
    u_b = u_tab[l].astype(BF16)
    vt_b = v_tab[l].T.astype(BF16)
    row = lambda v: v.reshape(1, -1)
    g1, g2, gf = row(g_norm1[l]), row(g_norm2[l]), row(g_final)
    cb, ba, bi, lm = row(conv_b[l]), row(b_a[l]), row(b_i[l]), row(lam[l])

    mod = _modulation(jnp.concatenate([c_prompt, c_sample], axis=0), w_mod[l], b_mod[l])
    mod_p = [m.reshape(nb, 1, D_MODEL) for m in jnp.split(mod[:nb], 6, axis=-1)]
    mod_s = [m.reshape(1, ns, D_MODEL) for m in jnp.split(mod[nb:], 6, axis=-1)]

    tile = TOKEN_TILE
    tpr = seq // tile
    xp = x_prompt.reshape(nb * seq, D_MODEL)
    cos_p, sin_p = _rope_tables(jnp.arange(seq))
    xr, gr, q, k, v, gates = _inproj(xp, mod_p[0], mod_p[1], g1, cos_p, sin_p, w_in_b, tile, tpr, tpr)
    hg, rnn_p = _rnn_prompt(xr, gr, conv_w[l], cb, wa_d, ba, wi_d, bi, lm, nb, seq, tile)
    o = _attn_prompt(sinks[l], q, k, v, nb, seq)
    x1, h2t = _merge(hg, o, gates, xp, mod_p[2], mod_p[3], mod_p[4], g2, wr_b, wat_b, wo_b, tile, tpr)
    y_p = _channel_mixer(h2t, x1, mod_p[5], gf, wq_t, keys_b, u_b, vt_b, tile, PEER_TOKEN_TILE, tpr)
    y_prompt = y_p.reshape(nb, seq, D_MODEL)
    conv_prompt = xr.reshape(nb, seq, D_RNN)[:, seq - (CONV_W - 1):][None]
    rnn_prompt = rnn_p.reshape(1, nb, D_RNN)
    nbuf = min(WINDOW, seq)
    win_k_prompt = k.reshape(nb, seq, N_KV, HEAD_DIM)[:, seq - nbuf:][None]
    win_v_prompt = v.reshape(nb, seq, N_KV, HEAD_DIM)[:, seq - nbuf:][None]

    xs = x_sample.reshape(ns, D_MODEL)
    cos_s, sin_s = _rope_tables(jnp.full((ns,), PAST_LEN))
    xr_s, gr_s, q_s, k_s, v_s, gates_s = _inproj(xs, mod_s[0], mod_s[1], g1, cos_s, sin_s, w_in_b, ns, 1, 1)
    sc = state_conv[l]
    hg_s, h_s = _rnn_sample(xr_s, gr_s, sc[:, 0], sc[:, 1], sc[:, 2], state_rnn[l],
                            conv_w[l], cb, wa_d, ba, wi_d, bi, lm)
    head_group = (jnp.arange(N_HEADS)[:, None] // GROUP == jnp.arange(N_KV)[None, :])
    qx = jnp.where(head_group[None, :, :, None], q_s.reshape(ns, N_HEADS, 1, HEAD_DIM), 0)
    qx = qx.reshape(ns, N_HEADS, KV_W).astype(BF16)
    ck = cache_win_k[l].reshape(ns, wbuf, KV_W)
    cv = cache_win_v[l].reshape(ns, wbuf, KV_W)
    ox, wk_s, wv_s = _attn_sample(sinks[l].reshape(N_HEADS, 1), qx, ck, cv,
                                  k_s.reshape(ns, 1, KV_W), v_s.reshape(ns, 1, KV_W))
    o_s = jnp.where(head_group[None, :, :, None], ox.reshape(ns, N_HEADS, N_KV, HEAD_DIM), 0.0).sum(axis=2)
    o_s = o_s.reshape(ns, Q_W).astype(BF16)
    x1_s, h2t_s = _merge(hg_s, o_s, gates_s, xs, mod_s[2], mod_s[3], mod_s[4], g2, wr_b, wat_b, wo_b, ns, 1)
    y_s = _channel_mixer(h2t_s, x1_s, mod_s[5], gf, wq_t, keys_b, u_b, vt_b, ns, ns, 1)
    y_sample = y_s.reshape(ns, 1, D_MODEL)
    conv_sample = jnp.concatenate([sc[:, 1:], xr_s[:, None, :]], axis=1)[None]
    rnn_sample = h_s[None]
    win_k_sample = wk_s.reshape(1, ns, wbuf, N_KV, HEAD_DIM)
    win_v_sample = wv_s.reshape(1, ns, wbuf, N_KV, HEAD_DIM)

    return (y_prompt, y_sample, conv_prompt, rnn_prompt, win_k_prompt, win_v_prompt,
            conv_sample, rnn_sample, win_k_sample, win_v_sample)
```

```python
import functools

import jax
import jax.numpy as jnp
from jax import lax
from jax.experimental import pallas as pl
from jax.experimental.pallas import tpu as pltpu

F32 = jnp.float32
BF16 = jnp.bfloat16

D_MODEL = 1024
PAST_LEN = 8192
D_RNN = 1280
RNN_BLOCKS = 16
RNN_BW = D_RNN // RNN_BLOCKS
CONV_W = 4
LRU_C = 8.0
N_HEADS = 16
N_KV = 4
HEAD_DIM = 64
GROUP = N_HEADS // N_KV
WINDOW = 128
ROPE_THETA = 10000.0
Q_W = N_HEADS * HEAD_DIM
KV_W = N_KV * HEAD_DIM
PEER_HEADS = 8
N_KEYS = 128
N_EXPERTS = N_KEYS * N_KEYS
PEER_DK = 256
PEER_DHALF = PEER_DK // 2
PEER_TOPK = 16
EPS = 1e-6
NEG = -1e30
GATE_W = 2 * D_MODEL

LANES = 128
SUBLANES = 8
MIB = 1024 * 1024

TOKEN_TILE = 256
PEER_TOKEN_TILE = 512
PEER_UNIT_ROWS = 32
PEER_EXPERT_BLOCK = 2048
PEER_A_PER_BLOCK = PEER_EXPERT_BLOCK // N_KEYS


def _params(semantics, vmem_mib):
    return pltpu.CompilerParams(dimension_semantics=semantics, vmem_limit_bytes=vmem_mib * MIB)


def _gelu(x):
    return 0.5 * x * (1.0 + lax.erf(x * (2.0 ** -0.5)))


def _rmsnorm(x, g):
    return x * lax.rsqrt(jnp.mean(x * x, axis=-1, keepdims=True) + EPS) * g


def _mod_kernel(c_ref, w_ref, b_ref, o_ref):
    c = c_ref[...]
    s = (c * jax.nn.sigmoid(c)).astype(BF16)
    o_ref[...] = jnp.dot(s, w_ref[...].astype(BF16), preferred_element_type=F32) + b_ref[...]


def _modulation(c_all, w_mod, b_mod):
    rows = c_all.shape[0]
    nblk = w_mod.shape[1] // D_MODEL
    return pl.pallas_call(
        _mod_kernel,
        grid=(nblk,),
        in_specs=[pl.BlockSpec((rows, D_MODEL), lambda j: (0, 0)),
                  pl.BlockSpec((D_MODEL, D_MODEL), lambda j: (0, j)),
                  pl.BlockSpec((1, D_MODEL), lambda j: (0, j))],
        out_specs=pl.BlockSpec((rows, D_MODEL), lambda j: (0, j)),
        out_shape=jax.ShapeDtypeStruct((rows, w_mod.shape[1]), F32),
        compiler_params=_params(("arbitrary",), 32),
    )(c_all, w_mod, b_mod.reshape(1, -1))


def _rope(x, cos, sin_signed):
    width = x.shape[1]
    reps = width // LANES
    cosf = jnp.concatenate([cos] * reps, axis=1)
    sinf = jnp.concatenate([sin_signed] * reps, axis=1)
    half = HEAD_DIM // 2
    upper = pltpu.roll(x, width - half, axis=1)
    lower = pltpu.roll(x, half, axis=1)
    lane = lax.broadcasted_iota(jnp.int32, x.shape, 1)
    rot = jnp.where((lane & half) == 0, upper, lower)
    return x * cosf + rot * sinf


def _inproj_kernel(x_ref, sh_ref, sc_ref, g_ref, cos_ref, sin_ref, w_ref,
                   xr_ref, gr_ref, q_ref, k_ref, v_ref, gt_ref):
    x = x_ref[...]
    hn = (_rmsnorm(x, g_ref[...]) * (1.0 + sc_ref[0]) + sh_ref[0]).astype(BF16)

    def proj(c0, width):
        return jnp.dot(hn, w_ref[:, c0:c0 + width], preferred_element_type=F32)

    c = 0
    xr_ref[...] = proj(c, D_RNN); c += D_RNN
    gr_ref[...] = proj(c, D_RNN); c += D_RNN
    q_ref[...] = _rope(proj(c, Q_W), cos_ref[...], sin_ref[...]).astype(BF16); c += Q_W
    k_ref[...] = _rope(proj(c, KV_W), cos_ref[...], sin_ref[...]); c += KV_W
    v_ref[...] = proj(c, KV_W); c += KV_W
    gt_ref[...] = proj(c, GATE_W)


def _inproj(x2d, sh, sc, g, cos, sin_signed, w_in_bf16, tile, tiles_per_row, rope_tiles):
    n = x2d.shape[0]
    rmod = sh.shape[1]
    d_in = w_in_bf16.shape[1]
    tok = lambda width: pl.BlockSpec((tile, width), lambda i: (i, 0))
    mod = pl.BlockSpec((1, rmod, D_MODEL), lambda i: (i // tiles_per_row, 0, 0))
    rope = pl.BlockSpec((tile, LANES), lambda i: (i % rope_tiles, 0))
    return pl.pallas_call(
        _inproj_kernel,
        grid=(n // tile,),
        in_specs=[tok(D_MODEL), mod, mod, pl.BlockSpec((1, D_MODEL), lambda i: (0, 0)), rope, rope,
                  pl.BlockSpec((D_MODEL, d_in), lambda i: (0, 0), pipeline_mode=pl.Buffered(1))],
        out_specs=[tok(D_RNN), tok(D_RNN), tok(Q_W), tok(KV_W), tok(KV_W), tok(GATE_W)],
        out_shape=[jax.ShapeDtypeStruct((n, D_RNN), F32), jax.ShapeDtypeStruct((n, D_RNN), F32),
                   jax.ShapeDtypeStruct((n, Q_W), BF16), jax.ShapeDtypeStruct((n, KV_W), F32),
                   jax.ShapeDtypeStruct((n, KV_W), F32), jax.ShapeDtypeStruct((n, GATE_W), F32)],
        compiler_params=_params(("arbitrary",), 48),
    )(x2d, sh, sc, g, cos, sin_signed, w_in_bf16)


def _log_sigmoid(x):
    return -(jnp.maximum(-x, 0.0) + jnp.log1p(jnp.exp(-jnp.abs(x))))


def _lru_coeffs(xc, wa_ref, ba_ref, wi_ref, bi_ref, lam_ref):
    xb = xc.astype(BF16)
    r = jax.nn.sigmoid(jnp.dot(xb, wa_ref[...], preferred_element_type=F32) + ba_ref[...])
    i = jax.nn.sigmoid(jnp.dot(xb, wi_ref[...], preferred_element_type=F32) + bi_ref[...])
    log_a = LRU_C * r * _log_sigmoid(lam_ref[...])
    a = jnp.exp(log_a)
    mult = jnp.sqrt(-jnp.tanh(log_a) * (1.0 + a * a))
    return a, mult * (i * xc)


def _shift_rows(x, s, fill):
    rows = x.shape[0]
    if s % SUBLANES == 0:
        return jnp.concatenate([jnp.full((s, x.shape[1]), fill, x.dtype), x[:rows - s]], axis=0)
    rolled = pltpu.roll(x, s, axis=0)
    row = lax.broadcasted_iota(jnp.int32, x.shape, 0)
    return jnp.where(row < s, fill, rolled)


def _rnn_prompt_kernel(xr_ref, gr_ref, cw_ref, cb_ref, wa_ref, ba_ref, wi_ref, bi_ref, lam_ref,
                       hg_ref, hlast_ref, tail_ref, hc_ref):
    t = pl.program_id(1)
    tile = xr_ref.shape[0]

    @pl.when(t == 0)
    def _():
        tail_ref[...] = jnp.zeros_like(tail_ref)
        hc_ref[...] = jnp.zeros_like(hc_ref)

    xr = xr_ref[...]
    cat = jnp.concatenate([tail_ref[...], xr], axis=0)
    xc = cb_ref[...] + cw_ref[CONV_W - 1:CONV_W, :] * xr
    for s in range(1, CONV_W):
        shifted = pltpu.roll(cat, s, axis=0)[SUBLANES:, :]
        xc = xc + cw_ref[CONV_W - 1 - s:CONV_W - s, :] * shifted
    tail_ref[...] = xr[tile - SUBLANES:, :]

    a, b = _lru_coeffs(xc, wa_ref, ba_ref, wi_ref, bi_ref, lam_ref)
    s = 1
    while s < tile:
        a_prev = _shift_rows(a, s, 1.0)
        b_prev = _shift_rows(b, s, 0.0)
        b = a * b_prev + b
        a = a * a_prev
        s *= 2
    h = a * hc_ref[0:1, :] + b
    hlast = h[tile - 1:tile, :]
    hc_ref[...] = jnp.broadcast_to(hlast, hc_ref.shape)
    hlast_ref[0] = hlast
    hg_ref[...] = (h * _gelu(gr_ref[...])).astype(BF16)


def _rnn_prompt(xr, gr, conv_w, conv_b, wa, ba, wi, bi, lam, nb, seq, tile):
    n = xr.shape[0]
    tpr = seq // tile
    tok = pl.BlockSpec((tile, D_RNN), lambda b, t: (b * tpr + t, 0))
    row = pl.BlockSpec((1, D_RNN), lambda b, t: (0, 0))
    sq = pl.BlockSpec((D_RNN, D_RNN), lambda b, t: (0, 0))
    return pl.pallas_call(
        _rnn_prompt_kernel,
        grid=(nb, tpr),
        in_specs=[tok, tok, pl.BlockSpec((CONV_W, D_RNN), lambda b, t: (0, 0)), row, sq, row, sq, row, row],
        out_specs=[tok, pl.BlockSpec((1, 1, D_RNN), lambda b, t: (b, 0, 0))],
        out_shape=[jax.ShapeDtypeStruct((n, D_RNN), BF16), jax.ShapeDtypeStruct((nb, 1, D_RNN), F32)],
        scratch_shapes=[pltpu.VMEM((SUBLANES, D_RNN), F32), pltpu.VMEM((SUBLANES, D_RNN), F32)],
        compiler_params=_params(("arbitrary", "arbitrary"), 48),
    )(xr, gr, conv_w, conv_b, wa, ba, wi, bi, lam)


def _rnn_sample_kernel(xr_ref, gr_ref, c0_ref, c1_ref, c2_ref, h0_ref, cw_ref, cb_ref,
                       wa_ref, ba_ref, wi_ref, bi_ref, lam_ref, hg_ref, h_ref):
    xc = (cb_ref[...] + cw_ref[0:1, :] * c0_ref[...] + cw_ref[1:2, :] * c1_ref[...]
          + cw_ref[2:3, :] * c2_ref[...] + cw_ref[3:4, :] * xr_ref[...])
    a, b = _lru_coeffs(xc, wa_ref, ba_ref, wi_ref, bi_ref, lam_ref)
    h = b + a * h0_ref[...]
    h_ref[...] = h
    hg_ref[...] = (h * _gelu(gr_ref[...])).astype(BF16)


def _rnn_sample(xr, gr, c0, c1, c2, h0, conv_w, conv_b, wa, ba, wi, bi, lam):
    n = xr.shape[0]
    return pl.pallas_call(
        _rnn_sample_kernel,
        out_shape=[jax.ShapeDtypeStruct((n, D_RNN), BF16), jax.ShapeDtypeStruct((n, D_RNN), F32)],
        compiler_params=pltpu.CompilerParams(vmem_limit_bytes=48 * MIB),
    )(xr, gr, c0, c1, c2, h0, conv_w, conv_b, wa, ba, wi, bi, lam)


def _attn_prompt_kernel(sink_ref, q_ref, kp_ref, kc_ref, vp_ref, vc_ref, o_ref):
    j = pl.program_id(1)
    kk = jnp.concatenate([kp_ref[...], kc_ref[...]], axis=0).astype(BF16)
    vv = jnp.concatenate([vp_ref[...], vc_ref[...]], axis=0).astype(BF16)
    qi = lax.broadcasted_iota(jnp.int32, (WINDOW, 2 * WINDOW), 0)
    ci = lax.broadcasted_iota(jnp.int32, (WINDOW, 2 * WINDOW), 1)
    dist = qi + WINDOW - ci
    kpos = ci + (j - 1) * WINDOW
    mask = (dist >= 0) & (dist <= WINDOW) & (kpos >= 0)
    q = q_ref[...]
    outs = []
    for h in range(N_HEADS):
        g = h // GROUP
        qh = q[:, h * HEAD_DIM:(h + 1) * HEAD_DIM]
        kh = kk[:, g * HEAD_DIM:(g + 1) * HEAD_DIM]
        vh = vv[:, g * HEAD_DIM:(g + 1) * HEAD_DIM]
        s = lax.dot_general(qh, kh, (((1,), (1,)), ((), ())), preferred_element_type=F32) * (HEAD_DIM ** -0.5)
        s = jnp.where(mask, s, NEG)
        sink = sink_ref[h]
        m = jnp.maximum(jnp.max(s, axis=-1, keepdims=True), sink)
        e = jnp.exp(s - m)
        p = e / (jnp.sum(e, axis=-1, keepdims=True) + jnp.exp(sink - m))
        outs.append(jnp.dot(p.astype(BF16), vh, preferred_element_type=F32))
    o_ref[...] = jnp.concatenate(outs, axis=1).astype(BF16)


def _attn_prompt(sinks, q, k, v, nb, seq):
    n = q.shape[0]
    nblk = seq // WINDOW
    cur = lambda b, j: (b * nblk + j, 0)
    prev = lambda b, j: (b * nblk + jnp.maximum(j - 1, 0), 0)
    return pl.pallas_call(
        _attn_prompt_kernel,
        grid=(nb, nblk),
        in_specs=[pl.BlockSpec(memory_space=pltpu.SMEM),
                  pl.BlockSpec((WINDOW, Q_W), cur),
                  pl.BlockSpec((WINDOW, KV_W), prev), pl.BlockSpec((WINDOW, KV_W), cur),
                  pl.BlockSpec((WINDOW, KV_W), prev), pl.BlockSpec((WINDOW, KV_W), cur)],
        out_specs=pl.BlockSpec((WINDOW, Q_W), cur),
        out_shape=jax.ShapeDtypeStruct((n, Q_W), BF16),
        compiler_params=_params(("arbitrary", "arbitrary"), 32),
    )(sinks, q, k, k, v, v)


SAMPLE_ATTN_ROWS = 8


def _attn_sample_kernel(sink_ref, qx_ref, ck_ref, cv_ref, kn_ref, vn_ref, o_ref, wk_ref, wv_ref):
    sink = sink_ref[...]
    wbuf = ck_ref.shape[1]
    last = lax.broadcasted_iota(jnp.int32, (wbuf, KV_W), 0) == wbuf - 1
    for r in range(SAMPLE_ATTN_ROWS):
        qx = qx_ref[r]
        wk_ref[r] = jnp.where(last, kn_ref[r], pltpu.roll(ck_ref[r], wbuf - 1, axis=0))
        wv_ref[r] = jnp.where(last, vn_ref[r], pltpu.roll(cv_ref[r], wbuf - 1, axis=0))
        kb = ck_ref[r].astype(BF16)
        vb = cv_ref[r].astype(BF16)
        kn = kn_ref[r].astype(BF16).astype(F32)
        vn = vn_ref[r].astype(BF16).astype(F32)
        scale = HEAD_DIM ** -0.5
        s = lax.dot_general(qx, kb, (((1,), (1,)), ((), ())), preferred_element_type=F32) * scale
        s_new = jnp.sum(qx.astype(F32) * kn, axis=-1, keepdims=True) * scale
        m = jnp.maximum(jnp.maximum(jnp.max(s, axis=-1, keepdims=True), s_new), sink)
        e = jnp.exp(s - m)
        e_new = jnp.exp(s_new - m)
        den = jnp.sum(e, axis=-1, keepdims=True) + e_new + jnp.exp(sink - m)
        p = (e / den).astype(BF16)
        p_new = (e_new / den).astype(BF16).astype(F32)
        o_ref[r] = jnp.dot(p, vb, preferred_element_type=F32) + p_new * vn


def _attn_sample(sinks_col, qx, cache_k, cache_v, k_new, v_new):
    nb, wbuf = cache_k.shape[0], cache_k.shape[1]
    rows = SAMPLE_ATTN_ROWS
    blk = lambda d1, d2: pl.BlockSpec((rows, d1, d2), lambda i: (i, 0, 0))
    return pl.pallas_call(
        _attn_sample_kernel,
        grid=(nb // rows,),
        in_specs=[pl.BlockSpec((N_HEADS, 1), lambda i: (0, 0)), blk(N_HEADS, KV_W),
                  blk(wbuf, KV_W), blk(wbuf, KV_W), blk(1, KV_W), blk(1, KV_W)],
        out_specs=[blk(N_HEADS, KV_W), blk(wbuf, KV_W), blk(wbuf, KV_W)],
        out_shape=[jax.ShapeDtypeStruct((nb, N_HEADS, KV_W), F32),
                   jax.ShapeDtypeStruct((nb, wbuf, KV_W), F32), jax.ShapeDtypeStruct((nb, wbuf, KV_W), F32)],
        compiler_params=_params(("arbitrary",), 32),
    )(sinks_col, qx, cache_k, cache_v, k_new, v_new)


def _merge_kernel(hg_ref, o_ref, gt_ref, x_ref, g1_ref, sh_ref, sc_ref, gn_ref, wr_ref, wa_ref, wo_ref,
                  x1_ref, h2t_ref):
    y_r = jnp.dot(hg_ref[...], wr_ref[...], preferred_element_type=F32)
    y_a = jnp.dot(o_ref[...], wa_ref[...], preferred_element_type=F32)
    merged = (jax.nn.sigmoid(gt_ref[:, :D_MODEL]) * y_r + jax.nn.sigmoid(gt_ref[:, D_MODEL:]) * y_a)
    x1 = x_ref[...] + g1_ref[0] * jnp.dot(merged.astype(BF16), wo_ref[...], preferred_element_type=F32)
    x1_ref[...] = x1
    h2 = _rmsnorm(x1, gn_ref[...]) * (1.0 + sc_ref[0]) + sh_ref[0]
    h2t_ref[...] = h2.T.astype(BF16)


def _merge(hg, o, gates, x2d, gt1, sh2, sc2, g2, wr, wa, wo, tile, tiles_per_row):
    n = x2d.shape[0]
    rmod = gt1.shape[1]
    tok = lambda width: pl.BlockSpec((tile, width), lambda i: (i, 0))
    mod = pl.BlockSpec((1, rmod, D_MODEL), lambda i: (i // tiles_per_row, 0, 0))
    full = lambda a: pl.BlockSpec(a.shape, lambda i: (0, 0))
    return pl.pallas_call(
        _merge_kernel,
        grid=(n // tile,),
        in_specs=[tok(D_RNN), tok(Q_W), tok(GATE_W), tok(D_MODEL), mod, mod, mod,
                  pl.BlockSpec((1, D_MODEL), lambda i: (0, 0)), full(wr), full(wa), full(wo)],
        out_specs=[tok(D_MODEL), pl.BlockSpec((D_MODEL, tile), lambda i: (0, i))],
        out_shape=[jax.ShapeDtypeStruct((n, D_MODEL), F32), jax.ShapeDtypeStruct((D_MODEL, n), BF16)],
        compiler_params=_params(("arbitrary",), 48),
    )(hg, o, gates, x2d, gt1, sh2, sc2, g2, wr, wa, wo)


def _vmax(a, b):
    if a is None:
        return b
    if b is None:
        return a
    return jnp.maximum(a, b)


def _vmin(a, b):
    if a is None or b is None:
        return None
    return jnp.minimum(a, b)


def _compare_exchange(xs, i, j):
    hi, lo = _vmax(xs[i], xs[j]), _vmin(xs[i], xs[j])
    xs[i], xs[j] = hi, lo


def _sort_pairs(n):
    pairs = []

    def merge(lo, hi, r):
        step = r * 2
        if step < hi - lo:
            merge(lo, hi, step)
            merge(lo + r, hi, step)
            for i in range(lo + r, hi - r, step):
                pairs.append((i, i + r))
        else:
            pairs.append((lo, lo + r))

    def sort(lo, hi):
        if hi - lo >= 1:
            mid = lo + (hi - lo) // 2
            sort(lo, mid)
            sort(mid + 1, hi)
            merge(lo, hi, 1)

    sort(0, n - 1)
    return pairs


_SORT16 = _sort_pairs(PEER_TOPK)


def _sort_desc(xs):
    xs = list(xs)
    for i, j in _SORT16:
        _compare_exchange(xs, i, j)
    return xs


def _bitonic_to_desc(xs):
    xs = list(xs)
    d = len(xs) // 2
    while d >= 1:
        for i in range(len(xs)):
            if (i & d) == 0:
                _compare_exchange(xs, i, i + d)
        d //= 2
    return xs


def _top_merge(xs, ys):
    k = len(xs)
    return _bitonic_to_desc([_vmax(xs[i], ys[k - 1 - i]) for i in range(k)])


def _top16_rows(s):
    rows = _sort_desc([s[r * SUBLANES:(r + 1) * SUBLANES, :] for r in range(N_KEYS // SUBLANES)])
    shift = SUBLANES // 2
    while shift >= 1:
        rows = _top_merge(rows, [pltpu.roll(x, shift, axis=0) for x in rows])
        shift //= 2
    return rows


def _top16_pair_sums(v1, v2):
    k = PEER_TOPK
    col = lambda j: [v1[i] + v2[j] for i in range(k // (j + 1))]
    g0 = col(0)
    row0_tail = [v1[0] + v2[j] for j in range(k // 2, k)]
    g1 = _bitonic_to_desc(col(1) + row0_tail[::-1])
    g2 = _sort_desc(col(2) + col(3) + col(4) + col(5) + col(6))
    g3 = col(7) + [None] * (k - 2)
    return _top_merge(_top_merge(g0, g1), _top_merge(g2, g3))


PEER_SPECIAL_RANKS = 4


def _threshold_chain(s1_tile, v2_head, tau):
    thr = jnp.full(s1_tile.shape, jnp.inf, F32)
    for v in v2_head:
        thr = jnp.where(s1_tile + v >= tau, v, thr)
    return thr


def _route_kernel(h2t_ref, wq_ref, keys_ref, thr_ref, c1_ref, s2_ref, e2_ref, qt_ref):
    qt_ref[...] = jnp.dot(wq_ref[...], h2t_ref[...], preferred_element_type=F32)
    k = PEER_TOPK
    groups = h2t_ref.shape[1] // LANES
    heads_per_iter = SUBLANES // groups
    sub = lax.broadcasted_iota(jnp.int32, (SUBLANES, LANES), 0)
    generic = k // (PEER_SPECIAL_RANKS + 1)

    def head_group(it, carry):
        packed1, packed2, kept = [None] * k, [None] * k, []
        for hh in range(heads_per_iter):
            h = it * heads_per_iter + hh
            base = pl.multiple_of(h * PEER_DK, PEER_DK)
            q1 = qt_ref[pl.ds(base, PEER_DHALF), :].astype(BF16)
            q2 = qt_ref[pl.ds(base + PEER_DHALF, PEER_DHALF), :].astype(BF16)
            s1 = jnp.dot(keys_ref[h, 0], q1, preferred_element_type=F32)
            s2 = jnp.dot(keys_ref[h, 1], q2, preferred_element_type=F32)
            c1_ref[h] = s1
            s2_ref[h] = s2
            for g in range(groups):
                v1 = _top16_rows(s1[:, g * LANES:(g + 1) * LANES])
                v2 = _top16_rows(s2[:, g * LANES:(g + 1) * LANES])
                slot = hh * groups + g
                for i in range(k):
                    packed1[i] = v1[i] if slot == 0 else jnp.where(sub == slot, v1[i], packed1[i])
                    packed2[i] = v2[i] if slot == 0 else jnp.where(sub == slot, v2[i], packed2[i])
                kept.append((h, g, v1[:PEER_SPECIAL_RANKS], v2[:generic]))

        top = _top16_pair_sums(packed1, packed2)
        tau = top[k - 1]
        z = jnp.ones_like(top[0])
        for f in top[1:]:
            z = z + jnp.exp(f - top[0])
        half_rz = 0.5 / z
        by_rank = [_threshold_chain(packed1[i], packed2[:k // (i + 1)], tau) for i in range(PEER_SPECIAL_RANKS)]

        for slot, (h, g, v1_top, v2_top) in enumerate(kept):
            unpack = lambda x: jnp.broadcast_to(x[slot:slot + 1, :], (SUBLANES, LANES))
            tau_s, rank_s, rz_s = unpack(tau), [unpack(t) for t in by_rank], unpack(half_rz)
            lanes = pl.ds(g * LANES, LANES)
            for r in range(N_KEYS // SUBLANES):
                rows = pl.ds(r * SUBLANES, SUBLANES)
                s1_tile = c1_ref[h, rows, lanes]
                thr = _threshold_chain(s1_tile, v2_top, tau_s)
                for i in reversed(range(PEER_SPECIAL_RANKS)):
                    thr = jnp.where(s1_tile >= v1_top[i], rank_s[i], thr)
                thr_ref[h, rows, lanes] = thr
                c1_ref[h, rows, lanes] = jnp.exp(s1_tile - v1_top[0]) * rz_s
            e2_ref[h, :, lanes] = jnp.exp(s2_ref[h, :, lanes] - v2_top[0][0:1, :])
        return carry

    lax.fori_loop(0, PEER_HEADS // heads_per_iter, head_group, 0)


def _route(h2t, wq_t, keys_bf16, tile):
    n = h2t.shape[1]
    tab = pl.BlockSpec((PEER_HEADS, N_KEYS, tile), lambda i: (0, 0, i))
    tab_shape = jax.ShapeDtypeStruct((PEER_HEADS, N_KEYS, n), F32)
    return pl.pallas_call(
        _route_kernel,
        grid=(n // tile,),
        in_specs=[pl.BlockSpec((D_MODEL, tile), lambda i: (0, i)),
                  pl.BlockSpec(wq_t.shape, lambda i: (0, 0)),
                  pl.BlockSpec(keys_bf16.shape, lambda i: (0, 0, 0, 0))],
        out_specs=[tab, tab, tab, tab],
        out_shape=[tab_shape, tab_shape, tab_shape, tab_shape],
        scratch_shapes=[pltpu.VMEM((PEER_HEADS * PEER_DK, tile), F32)],
        compiler_params=_params(("arbitrary",), 48),
    )(h2t, wq_t, keys_bf16)


def _peer_kernel(h2t_ref, u_ref, vt_ref, thr_ref, c1_ref, s2_ref, e2_ref, x1_ref, g2_ref, gf_ref,
                 y_ref, out_ref, act_ref, wgt_ref):
    j = pl.program_id(1)
    tile = h2t_ref.shape[1]

    @pl.when(j == 0)
    def _():
        out_ref[...] = jnp.zeros_like(out_ref)

    act_ref[...] = jnp.dot(u_ref[...], h2t_ref[...], preferred_element_type=F32)

    def row_group(r, carry):
        r0 = pl.multiple_of(r * PEER_UNIT_ROWS, PEER_UNIT_ROWS)
        rows = pl.ds(r0, PEER_UNIT_ROWS)
        for a in range(PEER_A_PER_BLOCK):
            erows = pl.ds(a * N_KEYS + r0, PEER_UNIT_ROWS)
            gates = [jnp.zeros((PEER_UNIT_ROWS, LANES), F32) for _ in range(tile // LANES)]
            for h in range(PEER_HEADS):
                for g in range(tile // LANES):
                    sl = pl.ds(g * LANES, LANES)
                    sel = jnp.where(s2_ref[h, rows, sl] >= thr_ref[h, a:a + 1, sl], e2_ref[h, rows, sl], 0.0)
                    gates[g] = gates[g] + sel * c1_ref[h, a:a + 1, sl]
            for g in range(tile // LANES):
                sl = pl.ds(g * LANES, LANES)
                x = act_ref[erows, sl]
                wgt_ref[erows, sl] = (gates[g] * (x * (1.0 + lax.erf(x * (2.0 ** -0.5))))).astype(BF16)
        return carry

    lax.fori_loop(0, N_KEYS // PEER_UNIT_ROWS, row_group, 0)
    out_ref[...] += jnp.dot(vt_ref[...], wgt_ref[...], preferred_element_type=F32)

    @pl.when(j == pl.num_programs(1) - 1)
    def _():
        x2 = x1_ref[...] + g2_ref[0] * out_ref[...].T
        y_ref[...] = _rmsnorm(x2, gf_ref[...])


def _peer(h2t, u_bf16, vt_bf16, thr, c1, s2, e2, x1, gt2, g_final, tile, tiles_per_row):
    n = h2t.shape[1]
    eb = PEER_EXPERT_BLOCK
    rmod = gt2.shape[1]
    tab = lambda rows: pl.BlockSpec((PEER_HEADS, rows, tile), lambda i, j: (0, 0, i))
    ablk = pl.BlockSpec((PEER_HEADS, PEER_A_PER_BLOCK, tile), lambda i, j: (0, j, i))
    tok = pl.BlockSpec((tile, D_MODEL), lambda i, j: (i, 0))
    return pl.pallas_call(
        _peer_kernel,
        grid=(n // tile, N_EXPERTS // eb),
        in_specs=[pl.BlockSpec((D_MODEL, tile), lambda i, j: (0, i)),
                  pl.BlockSpec((eb, D_MODEL), lambda i, j: (j, 0)),
                  pl.BlockSpec((D_MODEL, eb), lambda i, j: (0, j)),
                  ablk, ablk, tab(N_KEYS), tab(N_KEYS), tok,
                  pl.BlockSpec((1, rmod, D_MODEL), lambda i, j: (i // tiles_per_row, 0, 0)),
                  pl.BlockSpec((1, D_MODEL), lambda i, j: (0, 0))],
        out_specs=tok,
        out_shape=jax.ShapeDtypeStruct((n, D_MODEL), F32),
        scratch_shapes=[pltpu.VMEM((D_MODEL, tile), F32), pltpu.VMEM((eb, tile), F32),
                        pltpu.VMEM((eb, tile), BF16)],
        compiler_params=_params(("arbitrary", "arbitrary"), 56),
    )(h2t, u_bf16, vt_bf16, thr, c1, s2, e2, x1, gt2, g_final)


def _rope_tables(pos):
    half = HEAD_DIM // 2
    inv = ROPE_THETA ** (-jnp.arange(half, dtype=F32) / half)
    ang = pos.astype(F32)[:, None] * inv[None, :]
    cos, sin = jnp.cos(ang), jnp.sin(ang)
    reps = LANES // HEAD_DIM
    return (jnp.tile(jnp.concatenate([cos, cos], axis=1), (1, reps)),
            jnp.tile(jnp.concatenate([-sin, sin], axis=1), (1, reps)))


def _block_diag(w):
    eye = jnp.eye(RNN_BLOCKS, dtype=w.dtype)
    return (eye[:, None, :, None] * w[:, :, None, :]).reshape(D_RNN, D_RNN)


def _channel_mixer(h2t, x1, gt2, g_final, wq_t, keys, u_bf16, vt_bf16, tile, peer_tile, peer_tiles_per_row):
    thr, c1, s2, e2 = _route(h2t, wq_t, keys, tile)
    return _peer(h2t, u_bf16, vt_bf16, thr, c1, s2, e2, x1, gt2, g_final, peer_tile, peer_tiles_per_row)


def kernel(x_prompt, x_sample, state_conv, state_rnn, cache_win_k, cache_win_v, c_prompt, c_sample, g_norm1, g_norm2, g_final, w_mod, b_mod, w_in, conv_w, conv_b, w_a, b_a, w_i, b_i, lam, w_rnn_out, sinks, w_attn_out, w_out, w_pq, sub_keys, u_tab, v_tab):
    nb, seq, _ = x_prompt.shape
    ns = x_sample.shape[0]
    wbuf = cache_win_k.shape[2]
    assert x_sample.shape[1] == 1 and g_norm1.shape[0] == 1
    assert seq % TOKEN_TILE == 0 and seq % WINDOW == 0 and ns % LANES == 0 and seq % PEER_TOKEN_TILE == 0
    l = 0

    w_in_b = w_in[l].astype(BF16)
    wa_d = _block_diag(w_a[l]).astype(BF16)
    wi_d = _block_diag(w_i[l]).astype(BF16)
    wr_b, wat_b, wo_b = w_rnn_out[l].astype(BF16), w_attn_out[l].astype(BF16), w_out[l].astype(BF16)
    wq_t = w_pq[l].T.astype(BF16)
    keys_b = sub_keys[l].astype(BF16)
    u_b = u_tab[l].astype(BF16)
    vt_b = v_tab[l].T.astype(BF16)
    row = lambda v: v.reshape(1, -1)
    g1, g2, gf = row(g_norm1[l]), row(g_norm2[l]), row(g_final)
    cb, ba, bi, lm = row(conv_b[l]), row(b_a[l]), row(b_i[l]), row(lam[l])

    mod = _modulation(jnp.concatenate([c_prompt, c_sample], axis=0), w_mod[l], b_mod[l])
    mod_p = [m.reshape(nb, 1, D_MODEL) for m in jnp.split(mod[:nb], 6, axis=-1)]
    mod_s = [m.reshape(1, ns, D_MODEL) for m in jnp.split(mod[nb:], 6, axis=-1)]

    tile = TOKEN_TILE
    tpr = seq // tile
    xp = x_prompt.reshape(nb * seq, D_MODEL)
    cos_p, sin_p = _rope_tables(jnp.arange(seq))
    xr, gr, q, k, v, gates = _inproj(xp, mod_p[0], mod_p[1], g1, cos_p, sin_p, w_in_b, tile, tpr, tpr)
    hg, rnn_p = _rnn_prompt(xr, gr, conv_w[l], cb, wa_d, ba, wi_d, bi, lm, nb, seq, tile)
    o = _attn_prompt(sinks[l], q, k, v, nb, seq)
    x1, h2t = _merge(hg, o, gates, xp, mod_p[2], mod_p[3], mod_p[4], g2, wr_b, wat_b, wo_b, tile, tpr)
    y_p = _channel_mixer(h2t, x1, mod_p[5], gf, wq_t, keys_b, u_b, vt_b, tile, PEER_TOKEN_TILE,
                         seq // PEER_TOKEN_TILE)
    y_prompt = y_p.reshape(nb, seq, D_MODEL)
    conv_prompt = xr.reshape(nb, seq, D_RNN)[:, seq - (CONV_W - 1):][None]
    rnn_prompt = rnn_p.reshape(1, nb, D_RNN)
    nbuf = min(WINDOW, seq)
    win_k_prompt = k.reshape(nb, seq, N_KV, HEAD_DIM)[:, seq - nbuf:][None]
    win_v_prompt = v.reshape(nb, seq, N_KV, HEAD_DIM)[:, seq - nbuf:][None]

    xs = x_sample.reshape(ns, D_MODEL)
    cos_s, sin_s = _rope_tables(jnp.full((ns,), PAST_LEN))
    xr_s, gr_s, q_s, k_s, v_s, gates_s = _inproj(xs, mod_s[0], mod_s[1], g1, cos_s, sin_s, w_in_b, ns, 1, 1)
    sc = state_conv[l]
    hg_s, h_s = _rnn_sample(xr_s, gr_s, sc[:, 0], sc[:, 1], sc[:, 2], state_rnn[l],
                            conv_w[l], cb, wa_d, ba, wi_d, bi, lm)
    head_group = (jnp.arange(N_HEADS)[:, None] // GROUP == jnp.arange(N_KV)[None, :])
    qx = jnp.where(head_group[None, :, :, None], q_s.reshape(ns, N_HEADS, 1, HEAD_DIM), 0)
    qx = qx.reshape(ns, N_HEADS, KV_W).astype(BF16)
    ck = cache_win_k[l].reshape(ns, wbuf, KV_W)
    cv = cache_win_v[l].reshape(ns, wbuf, KV_W)
    ox, wk_s, wv_s = _attn_sample(sinks[l].reshape(N_HEADS, 1), qx, ck, cv,
                                  k_s.reshape(ns, 1, KV_W), v_s.reshape(ns, 1, KV_W))
    o_s = jnp.where(head_group[None, :, :, None], ox.reshape(ns, N_HEADS, N_KV, HEAD_DIM), 0.0).sum(axis=2)
    o_s = o_s.reshape(ns, Q_W).astype(BF16)
    x1_s, h2t_s = _merge(hg_s, o_s, gates_s, xs, mod_s[2], mod_s[3], mod_s[4], g2, wr_b, wat_b, wo_b, ns, 1)
    y_s = _channel_mixer(h2t_s, x1_s, mod_s[5], gf, wq_t, keys_b, u_b, vt_b, ns, ns, 1)
    y_sample = y_s.reshape(ns, 1, D_MODEL)
    conv_sample = jnp.concatenate([sc[:, 1:], xr_s[:, None, :]], axis=1)[None]
    rnn_sample = h_s[None]
    win_k_sample = wk_s.reshape(1, ns, wbuf, N_KV, HEAD_DIM)
    win_v_sample = wv_s.reshape(1, ns, wbuf, N_KV, HEAD_DIM)

    return (y_prompt, y_sample, conv_prompt, rnn_prompt, win_k_prompt, win_v_prompt,
            conv_sample, rnn_sample, win_k_sample, win_v_sample)
```

```python
import functools

import jax
import jax.numpy as jnp
from jax import lax
from jax.experimental import pallas as pl
from jax.experimental.pallas import tpu as pltpu

F32 = jnp.float32
BF16 = jnp.bfloat16

D_MODEL = 1024
PAST_LEN = 8192
D_RNN = 1280
RNN_BLOCKS = 16
RNN_BW = D_RNN // RNN_BLOCKS
CONV_W = 4
LRU_C = 8.0
N_HEADS = 16
N_KV = 4
HEAD_DIM = 64
GROUP = N_HEADS // N_KV
WINDOW = 128
ROPE_THETA = 10000.0
Q_W = N_HEADS * HEAD_DIM
KV_W = N_KV * HEAD_DIM
PEER_HEADS = 8
N_KEYS = 128
N_EXPERTS = N_KEYS * N_KEYS
PEER_DK = 256
PEER_DHALF = PEER_DK // 2
PEER_TOPK = 16
EPS = 1e-6
NEG = -1e30
GATE_W = 2 * D_MODEL

LANES = 128
SUBLANES = 8
MIB = 1024 * 1024

TOKEN_TILE = 256
WIDE_TOKEN_TILE = 512
PEER_TOKEN_TILE = 512
PEER_UNIT_ROWS = 32
PEER_EXPERT_BLOCK = 2048
PEER_A_PER_BLOCK = PEER_EXPERT_BLOCK // N_KEYS


def _params(semantics, vmem_mib):
    return pltpu.CompilerParams(dimension_semantics=semantics, vmem_limit_bytes=vmem_mib * MIB)


def _gelu(x):
    return 0.5 * x * (1.0 + lax.erf(x * (2.0 ** -0.5)))


def _rmsnorm(x, g):
    return x * lax.rsqrt(jnp.mean(x * x, axis=-1, keepdims=True) + EPS) * g


def _mod_kernel(c_ref, w_ref, b_ref, o_ref):
    c = c_ref[...]
    s = (c * jax.nn.sigmoid(c)).astype(BF16)
    o_ref[...] = jnp.dot(s, w_ref[...].astype(BF16), preferred_element_type=F32) + b_ref[...]


def _modulation(c_all, w_mod, b_mod):
    rows = c_all.shape[0]
    nblk = w_mod.shape[1] // D_MODEL
    return pl.pallas_call(
        _mod_kernel,
        grid=(nblk,),
        in_specs=[pl.BlockSpec((rows, D_MODEL), lambda j: (0, 0)),
                  pl.BlockSpec((D_MODEL, D_MODEL), lambda j: (0, j)),
                  pl.BlockSpec((1, D_MODEL), lambda j: (0, j))],
        out_specs=pl.BlockSpec((rows, D_MODEL), lambda j: (0, j)),
        out_shape=jax.ShapeDtypeStruct((rows, w_mod.shape[1]), F32),
        compiler_params=_params(("arbitrary",), 32),
    )(c_all, w_mod, b_mod.reshape(1, -1))


def _rope(x, cos, sin_signed):
    width = x.shape[1]
    reps = width // LANES
    cosf = jnp.concatenate([cos] * reps, axis=1)
    sinf = jnp.concatenate([sin_signed] * reps, axis=1)
    half = HEAD_DIM // 2
    upper = pltpu.roll(x, width - half, axis=1)
    lower = pltpu.roll(x, half, axis=1)
    lane = lax.broadcasted_iota(jnp.int32, x.shape, 1)
    rot = jnp.where((lane & half) == 0, upper, lower)
    return x * cosf + rot * sinf


def _inproj_kernel(x_ref, sh_ref, sc_ref, g_ref, cos_ref, sin_ref, w_ref,
                   xr_ref, gr_ref, q_ref, k_ref, v_ref, gt_ref):
    x = x_ref[...]
    hn = (_rmsnorm(x, g_ref[...]) * (1.0 + sc_ref[0]) + sh_ref[0]).astype(BF16)

    def proj(c0, width):
        return jnp.dot(hn, w_ref[:, c0:c0 + width], preferred_element_type=F32)

    c = 0
    xr_ref[...] = proj(c, D_RNN); c += D_RNN
    gr_ref[...] = proj(c, D_RNN); c += D_RNN
    q_ref[...] = _rope(proj(c, Q_W), cos_ref[...], sin_ref[...]).astype(BF16); c += Q_W
    k_ref[...] = _rope(proj(c, KV_W), cos_ref[...], sin_ref[...]); c += KV_W
    v_ref[...] = proj(c, KV_W); c += KV_W
    gt_ref[...] = proj(c, GATE_W)


def _inproj(x2d, sh, sc, g, cos, sin_signed, w_in_bf16, tile, tiles_per_row, rope_tiles):
    n = x2d.shape[0]
    rmod = sh.shape[1]
    d_in = w_in_bf16.shape[1]
    tok = lambda width: pl.BlockSpec((tile, width), lambda i: (i, 0))
    mod = pl.BlockSpec((1, rmod, D_MODEL), lambda i: (i // tiles_per_row, 0, 0))
    rope = pl.BlockSpec((tile, LANES), lambda i: (i % rope_tiles, 0))
    return pl.pallas_call(
        _inproj_kernel,
        grid=(n // tile,),
        in_specs=[tok(D_MODEL), mod, mod, pl.BlockSpec((1, D_MODEL), lambda i: (0, 0)), rope, rope,
                  pl.BlockSpec((D_MODEL, d_in), lambda i: (0, 0), pipeline_mode=pl.Buffered(1))],
        out_specs=[tok(D_RNN), tok(D_RNN), tok(Q_W), tok(KV_W), tok(KV_W), tok(GATE_W)],
        out_shape=[jax.ShapeDtypeStruct((n, D_RNN), F32), jax.ShapeDtypeStruct((n, D_RNN), F32),
                   jax.ShapeDtypeStruct((n, Q_W), BF16), jax.ShapeDtypeStruct((n, KV_W), F32),
                   jax.ShapeDtypeStruct((n, KV_W), F32), jax.ShapeDtypeStruct((n, GATE_W), F32)],
        compiler_params=_params(("arbitrary",), 48),
    )(x2d, sh, sc, g, cos, sin_signed, w_in_bf16)


def _log_sigmoid(x):
    return -(jnp.maximum(-x, 0.0) + jnp.log1p(jnp.exp(-jnp.abs(x))))


def _lru_coeffs(xc, wa_ref, ba_ref, wi_ref, bi_ref, lam_ref):
    xb = xc.astype(BF16)
    r = jax.nn.sigmoid(jnp.dot(xb, wa_ref[...], preferred_element_type=F32) + ba_ref[...])
    i = jax.nn.sigmoid(jnp.dot(xb, wi_ref[...], preferred_element_type=F32) + bi_ref[...])
    log_a = LRU_C * r * _log_sigmoid(lam_ref[...])
    a = jnp.exp(log_a)
    m = 1.0 - a * a
    mult = m * lax.rsqrt(jnp.maximum(m, 1e-30))
    return a, mult * (i * xc)


def _shift_rows_in_blocks(x, s, fill):
    rolled = pltpu.roll(x, s, axis=1)
    row = lax.broadcasted_iota(jnp.int32, x.shape, 1)
    return jnp.where(row < s, fill, rolled)


def _rnn_prompt_kernel(xr_ref, gr_ref, cw_ref, cb_ref, wa_ref, ba_ref, wi_ref, bi_ref, lam_ref,
                       hg_ref, hlast_ref, tail_ref, hc_ref):
    t = pl.program_id(1)
    tile = xr_ref.shape[0]

    @pl.when(t == 0)
    def _():
        tail_ref[...] = jnp.zeros_like(tail_ref)
        hc_ref[...] = jnp.zeros_like(hc_ref)

    xr = xr_ref[...]
    cat = jnp.concatenate([tail_ref[...], xr], axis=0)
    xc = cb_ref[...] + cw_ref[CONV_W - 1:CONV_W, :] * xr
    for s in range(1, CONV_W):
        shifted = pltpu.roll(cat, s, axis=0)[SUBLANES:, :]
        xc = xc + cw_ref[CONV_W - 1 - s:CONV_W - s, :] * shifted
    tail_ref[...] = xr[tile - SUBLANES:, :]

    a, b = _lru_coeffs(xc, wa_ref, ba_ref, wi_ref, bi_ref, lam_ref)
    nblocks = tile // SUBLANES
    a = a.reshape(nblocks, SUBLANES, D_RNN)
    b = b.reshape(nblocks, SUBLANES, D_RNN)
    s = 1
    while s < SUBLANES:
        a_prev = _shift_rows_in_blocks(a, s, 1.0)
        b_prev = _shift_rows_in_blocks(b, s, 0.0)
        b = a * b_prev + b
        a = a * a_prev
        s *= 2
    carry = hc_ref[0:1, :]
    blocks = []
    for k in range(nblocks):
        blocks.append(a[k] * carry + b[k])
        carry = a[k, SUBLANES - 1:, :] * carry + b[k, SUBLANES - 1:, :]
    h = jnp.concatenate(blocks, axis=0)
    hc_ref[...] = jnp.broadcast_to(carry, hc_ref.shape)
    hlast_ref[0] = carry
    hg_ref[...] = (h * _gelu(gr_ref[...])).astype(BF16)


def _rnn_prompt(xr, gr, conv_w, conv_b, wa, ba, wi, bi, lam, nb, seq, tile):
    n = xr.shape[0]
    tpr = seq // tile
    tok = pl.BlockSpec((tile, D_RNN), lambda b, t: (b * tpr + t, 0))
    row = pl.BlockSpec((1, D_RNN), lambda b, t: (0, 0))
    sq = pl.BlockSpec((D_RNN, D_RNN), lambda b, t: (0, 0))
    return pl.pallas_call(
        _rnn_prompt_kernel,
        grid=(nb, tpr),
        in_specs=[tok, tok, pl.BlockSpec((CONV_W, D_RNN), lambda b, t: (0, 0)), row, sq, row, sq, row, row],
        out_specs=[tok, pl.BlockSpec((1, 1, D_RNN), lambda b, t: (b, 0, 0))],
        out_shape=[jax.ShapeDtypeStruct((n, D_RNN), BF16), jax.ShapeDtypeStruct((nb, 1, D_RNN), F32)],
        scratch_shapes=[pltpu.VMEM((SUBLANES, D_RNN), F32), pltpu.VMEM((SUBLANES, D_RNN), F32)],
        compiler_params=_params(("arbitrary", "arbitrary"), 48),
    )(xr, gr, conv_w, conv_b, wa, ba, wi, bi, lam)


def _rnn_sample_kernel(xr_ref, gr_ref, c0_ref, c1_ref, c2_ref, h0_ref, cw_ref, cb_ref,
                       wa_ref, ba_ref, wi_ref, bi_ref, lam_ref, hg_ref, h_ref):
    xc = (cb_ref[...] + cw_ref[0:1, :] * c0_ref[...] + cw_ref[1:2, :] * c1_ref[...]
          + cw_ref[2:3, :] * c2_ref[...] + cw_ref[3:4, :] * xr_ref[...])
    a, b = _lru_coeffs(xc, wa_ref, ba_ref, wi_ref, bi_ref, lam_ref)
    h = b + a * h0_ref[...]
    h_ref[...] = h
    hg_ref[...] = (h * _gelu(gr_ref[...])).astype(BF16)


def _rnn_sample(xr, gr, c0, c1, c2, h0, conv_w, conv_b, wa, ba, wi, bi, lam):
    n = xr.shape[0]
    return pl.pallas_call(
        _rnn_sample_kernel,
        out_shape=[jax.ShapeDtypeStruct((n, D_RNN), BF16), jax.ShapeDtypeStruct((n, D_RNN), F32)],
        compiler_params=pltpu.CompilerParams(vmem_limit_bytes=48 * MIB),
    )(xr, gr, c0, c1, c2, h0, conv_w, conv_b, wa, ba, wi, bi, lam)


ATTN_BLOCKS_PER_STEP = 2


def _attn_prompt_kernel(sink_ref, q_ref, kp_ref, kc_ref, vp_ref, vc_ref, o_ref):
    step = pl.program_id(1)
    kk = jnp.concatenate([kp_ref[...], kc_ref[...]], axis=0).astype(BF16)
    vv = jnp.concatenate([vp_ref[...], vc_ref[...]], axis=0).astype(BF16)
    qi = lax.broadcasted_iota(jnp.int32, (WINDOW, 2 * WINDOW), 0)
    ci = lax.broadcasted_iota(jnp.int32, (WINDOW, 2 * WINDOW), 1)
    dist = qi + WINDOW - ci
    band = (dist >= 0) & (dist <= WINDOW)
    for u in range(ATTN_BLOCKS_PER_STEP):
        j = step * ATTN_BLOCKS_PER_STEP + u
        mask = band & (ci + (j - 1) * WINDOW >= 0)
        q = q_ref[pl.ds(u * WINDOW, WINDOW), :]
        outs = []
        for h in range(N_HEADS):
            g = h // GROUP
            qh = q[:, h * HEAD_DIM:(h + 1) * HEAD_DIM]
            kh = kk[u * WINDOW:(u + 2) * WINDOW, g * HEAD_DIM:(g + 1) * HEAD_DIM]
            vh = vv[u * WINDOW:(u + 2) * WINDOW, g * HEAD_DIM:(g + 1) * HEAD_DIM]
            s = lax.dot_general(qh, kh, (((1,), (1,)), ((), ())), preferred_element_type=F32) * (HEAD_DIM ** -0.5)
            s = jnp.where(mask, s, NEG)
            sink = sink_ref[h]
            m = jnp.maximum(jnp.max(s, axis=-1, keepdims=True), sink)
            e = jnp.exp(s - m)
            p = e / (jnp.sum(e, axis=-1, keepdims=True) + jnp.exp(sink - m))
            outs.append(jnp.dot(p.astype(BF16), vh, preferred_element_type=F32))
        o_ref[pl.ds(u * WINDOW, WINDOW), :] = jnp.concatenate(outs, axis=1).astype(BF16)


def _attn_prompt(sinks, q, k, v, nb, seq):
    n = q.shape[0]
    per = ATTN_BLOCKS_PER_STEP
    nstep = seq // (per * WINDOW)
    cur = lambda b, j: (b * nstep + j, 0)
    prev = lambda b, j: (jnp.maximum((b * nstep + j) * per - 1, b * nstep * per), 0)
    return pl.pallas_call(
        _attn_prompt_kernel,
        grid=(nb, nstep),
        in_specs=[pl.BlockSpec(memory_space=pltpu.SMEM),
                  pl.BlockSpec((per * WINDOW, Q_W), cur),
                  pl.BlockSpec((WINDOW, KV_W), prev), pl.BlockSpec((per * WINDOW, KV_W), cur),
                  pl.BlockSpec((WINDOW, KV_W), prev), pl.BlockSpec((per * WINDOW, KV_W), cur)],
        out_specs=pl.BlockSpec((per * WINDOW, Q_W), cur),
        out_shape=jax.ShapeDtypeStruct((n, Q_W), BF16),
        compiler_params=_params(("arbitrary", "arbitrary"), 32),
    )(sinks, q, k, k, v, v)


SAMPLE_ATTN_ROWS = 16


def _attn_sample_kernel(sink_ref, qx_ref, ck_ref, cv_ref, kn_ref, vn_ref, o_ref, wk_ref, wv_ref):
    sink = sink_ref[...]
    wbuf = ck_ref.shape[1]
    last = lax.broadcasted_iota(jnp.int32, (wbuf, KV_W), 0) == wbuf - 1
    for r in range(SAMPLE_ATTN_ROWS):
        qx = qx_ref[r]
        wk_ref[r] = jnp.where(last, kn_ref[r], pltpu.roll(ck_ref[r], wbuf - 1, axis=0))
        wv_ref[r] = jnp.where(last, vn_ref[r], pltpu.roll(cv_ref[r], wbuf - 1, axis=0))
        kb = ck_ref[r].astype(BF16)
        vb = cv_ref[r].astype(BF16)
        kn = kn_ref[r].astype(BF16).astype(F32)
        vn = vn_ref[r].astype(BF16).astype(F32)
        scale = HEAD_DIM ** -0.5
        s = lax.dot_general(qx, kb, (((1,), (1,)), ((), ())), preferred_element_type=F32) * scale
        s_new = jnp.sum(qx.astype(F32) * kn, axis=-1, keepdims=True) * scale
        m = jnp.maximum(jnp.maximum(jnp.max(s, axis=-1, keepdims=True), s_new), sink)
        e = jnp.exp(s - m)
        e_new = jnp.exp(s_new - m)
        den = jnp.sum(e, axis=-1, keepdims=True) + e_new + jnp.exp(sink - m)
        p = (e / den).astype(BF16)
        p_new = (e_new / den).astype(BF16).astype(F32)
        o_ref[r] = jnp.dot(p, vb, preferred_element_type=F32) + p_new * vn


def _attn_sample(sinks_col, qx, cache_k, cache_v, k_new, v_new):
    nb, wbuf = cache_k.shape[0], cache_k.shape[1]
    rows = SAMPLE_ATTN_ROWS
    blk = lambda d1, d2: pl.BlockSpec((rows, d1, d2), lambda i: (i, 0, 0))
    return pl.pallas_call(
        _attn_sample_kernel,
        grid=(nb // rows,),
        in_specs=[pl.BlockSpec((N_HEADS, 1), lambda i: (0, 0)), blk(N_HEADS, KV_W),
                  blk(wbuf, KV_W), blk(wbuf, KV_W), blk(1, KV_W), blk(1, KV_W)],
        out_specs=[blk(N_HEADS, KV_W), blk(wbuf, KV_W), blk(wbuf, KV_W)],
        out_shape=[jax.ShapeDtypeStruct((nb, N_HEADS, KV_W), F32),
                   jax.ShapeDtypeStruct((nb, wbuf, KV_W), F32), jax.ShapeDtypeStruct((nb, wbuf, KV_W), F32)],
        compiler_params=_params(("arbitrary",), 32),
    )(sinks_col, qx, cache_k, cache_v, k_new, v_new)


def _merge_kernel(hg_ref, o_ref, gt_ref, x_ref, g1_ref, sh_ref, sc_ref, gn_ref, wr_ref, wa_ref, wo_ref,
                  x1_ref, h2t_ref):
    y_r = jnp.dot(hg_ref[...], wr_ref[...], preferred_element_type=F32)
    y_a = jnp.dot(o_ref[...], wa_ref[...], preferred_element_type=F32)
    merged = (jax.nn.sigmoid(gt_ref[:, :D_MODEL]) * y_r + jax.nn.sigmoid(gt_ref[:, D_MODEL:]) * y_a)
    x1 = x_ref[...] + g1_ref[0] * jnp.dot(merged.astype(BF16), wo_ref[...], preferred_element_type=F32)
    x1_ref[...] = x1
    h2 = _rmsnorm(x1, gn_ref[...]) * (1.0 + sc_ref[0]) + sh_ref[0]
    h2t_ref[...] = h2.T.astype(BF16)


def _merge(hg, o, gates, x2d, gt1, sh2, sc2, g2, wr, wa, wo, tile, tiles_per_row):
    n = x2d.shape[0]
    rmod = gt1.shape[1]
    tok = lambda width: pl.BlockSpec((tile, width), lambda i: (i, 0))
    mod = pl.BlockSpec((1, rmod, D_MODEL), lambda i: (i // tiles_per_row, 0, 0))
    full = lambda a: pl.BlockSpec(a.shape, lambda i: (0, 0))
    return pl.pallas_call(
        _merge_kernel,
        grid=(n // tile,),
        in_specs=[tok(D_RNN), tok(Q_W), tok(GATE_W), tok(D_MODEL), mod, mod, mod,
                  pl.BlockSpec((1, D_MODEL), lambda i: (0, 0)), full(wr), full(wa), full(wo)],
        out_specs=[tok(D_MODEL), pl.BlockSpec((D_MODEL, tile), lambda i: (0, i))],
        out_shape=[jax.ShapeDtypeStruct((n, D_MODEL), F32), jax.ShapeDtypeStruct((D_MODEL, n), BF16)],
        compiler_params=_params(("arbitrary",), 48),
    )(hg, o, gates, x2d, gt1, sh2, sc2, g2, wr, wa, wo)


def _vmax(a, b):
    if a is None:
        return b
    if b is None:
        return a
    return jnp.maximum(a, b)


def _vmin(a, b):
    if a is None or b is None:
        return None
    return jnp.minimum(a, b)


def _compare_exchange(xs, i, j):
    hi, lo = _vmax(xs[i], xs[j]), _vmin(xs[i], xs[j])
    xs[i], xs[j] = hi, lo


def _sort_pairs(n):
    pairs = []

    def merge(lo, hi, r):
        step = r * 2
        if step < hi - lo:
            merge(lo, hi, step)
            merge(lo + r, hi, step)
            for i in range(lo + r, hi - r, step):
                pairs.append((i, i + r))
        else:
            pairs.append((lo, lo + r))

    def sort(lo, hi):
        if hi - lo >= 1:
            mid = lo + (hi - lo) // 2
            sort(lo, mid)
            sort(mid + 1, hi)
            merge(lo, hi, 1)

    sort(0, n - 1)
    return pairs


_SORT16 = _sort_pairs(PEER_TOPK)


def _sort_desc(xs):
    xs = list(xs)
    for i, j in _SORT16:
        _compare_exchange(xs, i, j)
    return xs


def _bitonic_to_desc(xs):
    xs = list(xs)
    d = len(xs) // 2
    while d >= 1:
        for i in range(len(xs)):
            if (i & d) == 0:
                _compare_exchange(xs, i, i + d)
        d //= 2
    return xs


def _top_merge(xs, ys):
    k = len(xs)
    return _bitonic_to_desc([_vmax(xs[i], ys[k - 1 - i]) for i in range(k)])


def _top16_rows(s):
    rows = _sort_desc([s[r * SUBLANES:(r + 1) * SUBLANES, :] for r in range(N_KEYS // SUBLANES)])
    shift = SUBLANES // 2
    while shift >= 1:
        rows = _top_merge(rows, [pltpu.roll(x, shift, axis=0) for x in rows])
        shift //= 2
    return rows


def _top16_pair_sums(v1, v2):
    k = PEER_TOPK
    col = lambda j: [v1[i] + v2[j] for i in range(k // (j + 1))]
    g0 = col(0)
    row0_tail = [v1[0] + v2[j] for j in range(k // 2, k)]
    g1 = _bitonic_to_desc(col(1) + row0_tail[::-1])
    g2 = _sort_desc(col(2) + col(3) + col(4) + col(5) + col(6))
    g3 = col(7) + [None] * (k - 2)
    return _top_merge(_top_merge(g0, g1), _top_merge(g2, g3))


PEER_SPECIAL_RANKS = 4


def _threshold_chain(s1_tile, v2_head, tau):
    thr = jnp.full(s1_tile.shape, jnp.inf, F32)
    for v in v2_head:
        thr = jnp.where(s1_tile + v >= tau, v, thr)
    return thr


def _route_kernel(h2t_ref, wq_ref, keys_ref, thr_ref, c1_ref, s2_ref, e2_ref, qt_ref):
    qt_ref[...] = jnp.dot(wq_ref[...], h2t_ref[...], preferred_element_type=F32)
    k = PEER_TOPK
    groups = h2t_ref.shape[1] // LANES
    heads_per_iter = SUBLANES // groups
    sub = lax.broadcasted_iota(jnp.int32, (SUBLANES, LANES), 0)
    generic = k // (PEER_SPECIAL_RANKS + 1)

    def head_group(it, carry):
        packed1, packed2, kept = [None] * k, [None] * k, []
        for hh in range(heads_per_iter):
            h = it * heads_per_iter + hh
            base = pl.multiple_of(h * PEER_DK, PEER_DK)
            q1 = qt_ref[pl.ds(base, PEER_DHALF), :].astype(BF16)
            q2 = qt_ref[pl.ds(base + PEER_DHALF, PEER_DHALF), :].astype(BF16)
            s1 = jnp.dot(keys_ref[h, 0], q1, preferred_element_type=F32)
            s2 = jnp.dot(keys_ref[h, 1], q2, preferred_element_type=F32)
            c1_ref[h] = s1
            s2_ref[h] = s2
            for g in range(groups):
                v1 = _top16_rows(s1[:, g * LANES:(g + 1) * LANES])
                v2 = _top16_rows(s2[:, g * LANES:(g + 1) * LANES])
                slot = hh * groups + g
                for i in range(k):
                    packed1[i] = v1[i] if slot == 0 else jnp.where(sub == slot, v1[i], packed1[i])
                    packed2[i] = v2[i] if slot == 0 else jnp.where(sub == slot, v2[i], packed2[i])
                kept.append((h, g, v1[:PEER_SPECIAL_RANKS], v2[:generic]))

        top = _top16_pair_sums(packed1, packed2)
        tau = top[k - 1]
        z = jnp.ones_like(top[0])
        for f in top[1:]:
            z = z + jnp.exp(f - top[0])
        half_rz = 0.5 / z
        by_rank = [_threshold_chain(packed1[i], packed2[:k // (i + 1)], tau) for i in range(PEER_SPECIAL_RANKS)]

        for slot, (h, g, v1_top, v2_top) in enumerate(kept):
            unpack = lambda x: jnp.broadcast_to(x[slot:slot + 1, :], (SUBLANES, LANES))
            tau_s, rank_s, rz_s = unpack(tau), [unpack(t) for t in by_rank], unpack(half_rz)
            lanes = pl.ds(g * LANES, LANES)
            for r in range(N_KEYS // SUBLANES):
                rows = pl.ds(r * SUBLANES, SUBLANES)
                s1_tile = c1_ref[h, rows, lanes]
                thr = _threshold_chain(s1_tile, v2_top, tau_s)
                for i in reversed(range(PEER_SPECIAL_RANKS)):
                    thr = jnp.where(s1_tile >= v1_top[i], rank_s[i], thr)
                thr_ref[h, rows, lanes] = thr
                c1_ref[h, rows, lanes] = jnp.exp(s1_tile - v1_top[0]) * rz_s
            e2_ref[h, :, lanes] = jnp.exp(s2_ref[h, :, lanes] - v2_top[0][0:1, :])
        return carry

    lax.fori_loop(0, PEER_HEADS // heads_per_iter, head_group, 0)


def _route(h2t, wq_t, keys_bf16, tile):
    n = h2t.shape[1]
    tab = pl.BlockSpec((PEER_HEADS, N_KEYS, tile), lambda i: (0, 0, i))
    tab_shape = jax.ShapeDtypeStruct((PEER_HEADS, N_KEYS, n), F32)
    return pl.pallas_call(
        _route_kernel,
        grid=(n // tile,),
        in_specs=[pl.BlockSpec((D_MODEL, tile), lambda i: (0, i)),
                  pl.BlockSpec(wq_t.shape, lambda i: (0, 0)),
                  pl.BlockSpec(keys_bf16.shape, lambda i: (0, 0, 0, 0))],
        out_specs=[tab, tab, tab, tab],
        out_shape=[tab_shape, tab_shape, tab_shape, tab_shape],
        scratch_shapes=[pltpu.VMEM((PEER_HEADS * PEER_DK, tile), F32)],
        compiler_params=_params(("arbitrary",), 48),
    )(h2t, wq_t, keys_bf16)


def _peer_kernel(h2t_ref, u_ref, vt_ref, thr_ref, c1_ref, s2_ref, e2_ref, x1_ref, g2_ref, gf_ref,
                 y_ref, out_ref, act_ref, wgt_ref):
    j = pl.program_id(1)
    tile = h2t_ref.shape[1]

    @pl.when(j == 0)
    def _():
        out_ref[...] = jnp.zeros_like(out_ref)

    act_ref[...] = jnp.dot(u_ref[...], h2t_ref[...], preferred_element_type=F32)

    def row_group(r, carry):
        r0 = pl.multiple_of(r * PEER_UNIT_ROWS, PEER_UNIT_ROWS)
        rows = pl.ds(r0, PEER_UNIT_ROWS)
        for a in range(PEER_A_PER_BLOCK):
            erows = pl.ds(a * N_KEYS + r0, PEER_UNIT_ROWS)
            gates = [jnp.zeros((PEER_UNIT_ROWS, LANES), F32) for _ in range(tile // LANES)]
            for h in range(PEER_HEADS):
                for g in range(tile // LANES):
                    sl = pl.ds(g * LANES, LANES)
                    sel = jnp.where(s2_ref[h, rows, sl] >= thr_ref[h, a:a + 1, sl], e2_ref[h, rows, sl], 0.0)
                    gates[g] = gates[g] + sel * c1_ref[h, a:a + 1, sl]
            for g in range(tile // LANES):
                sl = pl.ds(g * LANES, LANES)
                x = act_ref[erows, sl]
                wgt_ref[erows, sl] = (gates[g] * (x * (1.0 + lax.erf(x * (2.0 ** -0.5))))).astype(BF16)
        return carry

    lax.fori_loop(0, N_KEYS // PEER_UNIT_ROWS, row_group, 0)
    out_ref[...] += jnp.dot(vt_ref[...], wgt_ref[...], preferred_element_type=F32)

    @pl.when(j == pl.num_programs(1) - 1)
    def _():
        x2 = x1_ref[...] + g2_ref[0] * out_ref[...].T
        y_ref[...] = _rmsnorm(x2, gf_ref[...])


def _peer(h2t, u_bf16, vt_bf16, thr, c1, s2, e2, x1, gt2, g_final, tile, tiles_per_row):
    n = h2t.shape[1]
    eb = PEER_EXPERT_BLOCK
    rmod = gt2.shape[1]
    tab = lambda rows: pl.BlockSpec((PEER_HEADS, rows, tile), lambda i, j: (0, 0, i))
    ablk = pl.BlockSpec((PEER_HEADS, PEER_A_PER_BLOCK, tile), lambda i, j: (0, j, i))
    tok = pl.BlockSpec((tile, D_MODEL), lambda i, j: (i, 0))
    return pl.pallas_call(
        _peer_kernel,
        grid=(n // tile, N_EXPERTS // eb),
        in_specs=[pl.BlockSpec((D_MODEL, tile), lambda i, j: (0, i)),
                  pl.BlockSpec((eb, D_MODEL), lambda i, j: (j, 0)),
                  pl.BlockSpec((D_MODEL, eb), lambda i, j: (0, j)),
                  ablk, ablk, tab(N_KEYS), tab(N_KEYS), tok,
                  pl.BlockSpec((1, rmod, D_MODEL), lambda i, j: (i // tiles_per_row, 0, 0)),
                  pl.BlockSpec((1, D_MODEL), lambda i, j: (0, 0))],
        out_specs=tok,
        out_shape=jax.ShapeDtypeStruct((n, D_MODEL), F32),
        scratch_shapes=[pltpu.VMEM((D_MODEL, tile), F32), pltpu.VMEM((eb, tile), F32),
                        pltpu.VMEM((eb, tile), BF16)],
        compiler_params=_params(("arbitrary", "arbitrary"), 56),
    )(h2t, u_bf16, vt_bf16, thr, c1, s2, e2, x1, gt2, g_final)


def _rope_tables(pos):
    half = HEAD_DIM // 2
    inv = ROPE_THETA ** (-jnp.arange(half, dtype=F32) / half)
    ang = pos.astype(F32)[:, None] * inv[None, :]
    cos, sin = jnp.cos(ang), jnp.sin(ang)
    reps = LANES // HEAD_DIM
    return (jnp.tile(jnp.concatenate([cos, cos], axis=1), (1, reps)),
            jnp.tile(jnp.concatenate([-sin, sin], axis=1), (1, reps)))


def _block_diag(w):
    eye = jnp.eye(RNN_BLOCKS, dtype=w.dtype)
    return (eye[:, None, :, None] * w[:, :, None, :]).reshape(D_RNN, D_RNN)


def _channel_mixer(h2t, x1, gt2, g_final, wq_t, keys, u_bf16, vt_bf16, tile, peer_tile, peer_tiles_per_row):
    thr, c1, s2, e2 = _route(h2t, wq_t, keys, tile)
    return _peer(h2t, u_bf16, vt_bf16, thr, c1, s2, e2, x1, gt2, g_final, peer_tile, peer_tiles_per_row)


def kernel(x_prompt, x_sample, state_conv, state_rnn, cache_win_k, cache_win_v, c_prompt, c_sample, g_norm1, g_norm2, g_final, w_mod, b_mod, w_in, conv_w, conv_b, w_a, b_a, w_i, b_i, lam, w_rnn_out, sinks, w_attn_out, w_out, w_pq, sub_keys, u_tab, v_tab):
    nb, seq, _ = x_prompt.shape
    ns = x_sample.shape[0]
    wbuf = cache_win_k.shape[2]
    assert x_sample.shape[1] == 1 and g_norm1.shape[0] == 1
    assert seq % TOKEN_TILE == 0 and seq % WIDE_TOKEN_TILE == 0 and seq % (ATTN_BLOCKS_PER_STEP * WINDOW) == 0 and seq % WINDOW == 0 and ns % LANES == 0 and seq % PEER_TOKEN_TILE == 0
    l = 0

    w_in_b = w_in[l].astype(BF16)
    wa_d = _block_diag(w_a[l]).astype(BF16)
    wi_d = _block_diag(w_i[l]).astype(BF16)
    wr_b, wat_b, wo_b = w_rnn_out[l].astype(BF16), w_attn_out[l].astype(BF16), w_out[l].astype(BF16)
    wq_t = w_pq[l].T.astype(BF16)
    keys_b = sub_keys[l].astype(BF16)
    u_b = u_tab[l].astype(BF16)
    vt_b = v_tab[l].T.astype(BF16)
    row = lambda v: v.reshape(1, -1)
    g1, g2, gf = row(g_norm1[l]), row(g_norm2[l]), row(g_final)
    cb, ba, bi, lm = row(conv_b[l]), row(b_a[l]), row(b_i[l]), row(lam[l])

    mod = _modulation(jnp.concatenate([c_prompt, c_sample], axis=0), w_mod[l], b_mod[l])
    mod_p = [m.reshape(nb, 1, D_MODEL) for m in jnp.split(mod[:nb], 6, axis=-1)]
    mod_s = [m.reshape(1, ns, D_MODEL) for m in jnp.split(mod[nb:], 6, axis=-1)]

    tile = TOKEN_TILE
    tpr = seq // tile
    xp = x_prompt.reshape(nb * seq, D_MODEL)
    cos_p, sin_p = _rope_tables(jnp.arange(seq))
    wide = WIDE_TOKEN_TILE
    xr, gr, q, k, v, gates = _inproj(xp, mod_p[0], mod_p[1], g1, cos_p, sin_p, w_in_b, wide, seq // wide, seq // wide)
    hg, rnn_p = _rnn_prompt(xr, gr, conv_w[l], cb, wa_d, ba, wi_d, bi, lm, nb, seq, tile)
    o = _attn_prompt(sinks[l], q, k, v, nb, seq)
    x1, h2t = _merge(hg, o, gates, xp, mod_p[2], mod_p[3], mod_p[4], g2, wr_b, wat_b, wo_b, wide, seq // wide)
    y_p = _channel_mixer(h2t, x1, mod_p[5], gf, wq_t, keys_b, u_b, vt_b, wide, PEER_TOKEN_TILE,
                         seq // PEER_TOKEN_TILE)
    y_prompt = y_p.reshape(nb, seq, D_MODEL)
    conv_prompt = xr.reshape(nb, seq, D_RNN)[:, seq - (CONV_W - 1):][None]
    rnn_prompt = rnn_p.reshape(1, nb, D_RNN)
    nbuf = min(WINDOW, seq)
    win_k_prompt = k.reshape(nb, seq, N_KV, HEAD_DIM)[:, seq - nbuf:][None]
    win_v_prompt = v.reshape(nb, seq, N_KV, HEAD_DIM)[:, seq - nbuf:][None]

    xs = x_sample.reshape(ns, D_MODEL)
    cos_s, sin_s = _rope_tables(jnp.full((ns,), PAST_LEN))
    xr_s, gr_s, q_s, k_s, v_s, gates_s = _inproj(xs, mod_s[0], mod_s[1], g1, cos_s, sin_s, w_in_b, ns, 1, 1)
    sc = state_conv[l]
    hg_s, h_s = _rnn_sample(xr_s, gr_s, sc[:, 0], sc[:, 1], sc[:, 2], state_rnn[l],
                            conv_w[l], cb, wa_d, ba, wi_d, bi, lm)
    head_group = (jnp.arange(N_HEADS)[:, None] // GROUP == jnp.arange(N_KV)[None, :])
    qx = jnp.where(head_group[None, :, :, None], q_s.reshape(ns, N_HEADS, 1, HEAD_DIM), 0)
    qx = qx.reshape(ns, N_HEADS, KV_W).astype(BF16)
    ck = cache_win_k[l].reshape(ns, wbuf, KV_W)
    cv = cache_win_v[l].reshape(ns, wbuf, KV_W)
    ox, wk_s, wv_s = _attn_sample(sinks[l].reshape(N_HEADS, 1), qx, ck, cv,
                                  k_s.reshape(ns, 1, KV_W), v_s.reshape(ns, 1, KV_W))
    o_s = jnp.where(head_group[None, :, :, None], ox.reshape(ns, N_HEADS, N_KV, HEAD_DIM), 0.0).sum(axis=2)
    o_s = o_s.reshape(ns, Q_W).astype(BF16)
    x1_s, h2t_s = _merge(hg_s, o_s, gates_s, xs, mod_s[2], mod_s[3], mod_s[4], g2, wr_b, wat_b, wo_b, ns, 1)
    y_s = _channel_mixer(h2t_s, x1_s, mod_s[5], gf, wq_t, keys_b, u_b, vt_b, ns, ns, 1)
    y_sample = y_s.reshape(ns, 1, D_MODEL)
    conv_sample = jnp.concatenate([sc[:, 1:], xr_s[:, None, :]], axis=1)[None]
    rnn_sample = h_s[None]
    win_k_sample = wk_s.reshape(1, ns, wbuf, N_KV, HEAD_DIM)
    win_v_sample = wv_s.reshape(1, ns, wbuf, N_KV, HEAD_DIM)

    return (y_prompt, y_sample, conv_prompt, rnn_prompt, win_k_prompt, win_v_prompt,
            conv_sample, rnn_sample, win_k_sample, win_v_sample)
```

```python
import jax
import jax.numpy as jnp
from jax import lax
from jax.experimental import pallas as pl
from jax.experimental.pallas import tpu as pltpu

F32 = jnp.float32
BF16 = jnp.bfloat16

D_MODEL = 1024
PAST_LEN = 8192
D_RNN = 1280
RNN_BLOCKS = 16
RNN_BW = D_RNN // RNN_BLOCKS
CONV_W = 4
LRU_C = 8.0
N_HEADS = 16
N_KV = 4
HEAD_DIM = 64
GROUP = N_HEADS // N_KV
WINDOW = 128
ROPE_THETA = 10000.0
Q_W = N_HEADS * HEAD_DIM
KV_W = N_KV * HEAD_DIM
PEER_HEADS = 8
N_KEYS = 128
N_EXPERTS = N_KEYS * N_KEYS
PEER_DK = 256
PEER_DHALF = PEER_DK // 2
PEER_TOPK = 16
EPS = 1e-6
NEG = -1e30
GATE_W = 2 * D_MODEL

LANES = 128
SUBLANES = 8
MIB = 1024 * 1024

TOKEN_TILE = 256
WIDE_TOKEN_TILE = 512
PEER_TOKEN_TILE = 512
PEER_UNIT_ROWS = 32
PEER_EXPERT_BLOCK = 2048
PEER_A_PER_BLOCK = PEER_EXPERT_BLOCK // N_KEYS


def _params(semantics, vmem_mib):
    return pltpu.CompilerParams(dimension_semantics=semantics, vmem_limit_bytes=vmem_mib * MIB)


def _gelu(x):
    return 0.5 * x * (1.0 + lax.erf(x * (2.0 ** -0.5)))


def _rmsnorm(x, g):
    return x * lax.rsqrt(jnp.mean(x * x, axis=-1, keepdims=True) + EPS) * g


def _mod_kernel(c_ref, w_ref, b_ref, o_ref):
    c = c_ref[...]
    s = (c * jax.nn.sigmoid(c)).astype(BF16)
    o_ref[...] = jnp.dot(s, w_ref[...].astype(BF16), preferred_element_type=F32) + b_ref[...]


def _modulation(c_all, w_mod, b_mod):
    rows = c_all.shape[0]
    nblk = w_mod.shape[1] // D_MODEL
    return pl.pallas_call(
        _mod_kernel,
        grid=(nblk,),
        in_specs=[pl.BlockSpec((rows, D_MODEL), lambda j: (0, 0)),
                  pl.BlockSpec((D_MODEL, D_MODEL), lambda j: (0, j)),
                  pl.BlockSpec((1, D_MODEL), lambda j: (0, j))],
        out_specs=pl.BlockSpec((rows, D_MODEL), lambda j: (0, j)),
        out_shape=jax.ShapeDtypeStruct((rows, w_mod.shape[1]), F32),
        compiler_params=_params(("arbitrary",), 32),
    )(c_all, w_mod, b_mod.reshape(1, -1))


def _rope(x, cos, sin_signed):
    width = x.shape[1]
    reps = width // LANES
    cosf = jnp.concatenate([cos] * reps, axis=1)
    sinf = jnp.concatenate([sin_signed] * reps, axis=1)
    half = HEAD_DIM // 2
    upper = pltpu.roll(x, width - half, axis=1)
    lower = pltpu.roll(x, half, axis=1)
    lane = lax.broadcasted_iota(jnp.int32, x.shape, 1)
    rot = jnp.where((lane & half) == 0, upper, lower)
    return x * cosf + rot * sinf


def _inproj_kernel(x_ref, sh_ref, sc_ref, g_ref, cos_ref, sin_ref, w_ref,
                   xr_ref, gr_ref, q_ref, k_ref, v_ref, gt_ref):
    x = x_ref[...]
    hn = (_rmsnorm(x, g_ref[...]) * (1.0 + sc_ref[0]) + sh_ref[0]).astype(BF16)

    def proj(c0, width):
        return jnp.dot(hn, w_ref[:, c0:c0 + width], preferred_element_type=F32)

    c = 0
    xr_ref[...] = proj(c, D_RNN); c += D_RNN
    gr_ref[...] = proj(c, D_RNN); c += D_RNN
    q_ref[...] = _rope(proj(c, Q_W), cos_ref[...], sin_ref[...]).astype(BF16); c += Q_W
    k_ref[...] = _rope(proj(c, KV_W), cos_ref[...], sin_ref[...]); c += KV_W
    v_ref[...] = proj(c, KV_W); c += KV_W
    gt_ref[...] = proj(c, GATE_W)


def _inproj(x2d, sh, sc, g, cos, sin_signed, w_in_bf16, tile, tiles_per_row, rope_tiles):
    n = x2d.shape[0]
    rmod = sh.shape[1]
    d_in = w_in_bf16.shape[1]
    tok = lambda width: pl.BlockSpec((tile, width), lambda i: (i, 0))
    mod = pl.BlockSpec((1, rmod, D_MODEL), lambda i: (i // tiles_per_row, 0, 0))
    rope = pl.BlockSpec((tile, LANES), lambda i: (i % rope_tiles, 0))
    return pl.pallas_call(
        _inproj_kernel,
        grid=(n // tile,),
        in_specs=[tok(D_MODEL), mod, mod, pl.BlockSpec((1, D_MODEL), lambda i: (0, 0)), rope, rope,
                  pl.BlockSpec((D_MODEL, d_in), lambda i: (0, 0), pipeline_mode=pl.Buffered(1))],
        out_specs=[tok(D_RNN), tok(D_RNN), tok(Q_W), tok(KV_W), tok(KV_W), tok(GATE_W)],
        out_shape=[jax.ShapeDtypeStruct((n, D_RNN), F32), jax.ShapeDtypeStruct((n, D_RNN), F32),
                   jax.ShapeDtypeStruct((n, Q_W), BF16), jax.ShapeDtypeStruct((n, KV_W), F32),
                   jax.ShapeDtypeStruct((n, KV_W), F32), jax.ShapeDtypeStruct((n, GATE_W), F32)],
        compiler_params=_params(("arbitrary",), 48),
    )(x2d, sh, sc, g, cos, sin_signed, w_in_bf16)


def _log_sigmoid(x):
    return -(jnp.maximum(-x, 0.0) + jnp.log1p(jnp.exp(-jnp.abs(x))))


def _lru_coeffs(xc, wa_ref, ba_ref, wi_ref, bi_ref, lam_ref):
    xb = xc.astype(BF16)
    r = jax.nn.sigmoid(jnp.dot(xb, wa_ref[...], preferred_element_type=F32) + ba_ref[...])
    i = jax.nn.sigmoid(jnp.dot(xb, wi_ref[...], preferred_element_type=F32) + bi_ref[...])
    log_a = LRU_C * r * _log_sigmoid(lam_ref[...])
    a = jnp.exp(log_a)
    m = 1.0 - a * a
    mult = m * lax.rsqrt(jnp.maximum(m, 1e-30))
    return a, mult * (i * xc)


def _shift_rows_in_blocks(x, s, fill):
    rolled = pltpu.roll(x, s, axis=1)
    row = lax.broadcasted_iota(jnp.int32, x.shape, 1)
    return jnp.where(row < s, fill, rolled)


def _rnn_prompt_kernel(xr_ref, gr_ref, cw_ref, cb_ref, wa_ref, ba_ref, wi_ref, bi_ref, lam_ref,
                       hg_ref, hlast_ref, tail_ref, hc_ref):
    t = pl.program_id(1)
    tile = xr_ref.shape[0]

    @pl.when(t == 0)
    def _():
        tail_ref[...] = jnp.zeros_like(tail_ref)
        hc_ref[...] = jnp.zeros_like(hc_ref)

    xr = xr_ref[...]
    cat = jnp.concatenate([tail_ref[...], xr], axis=0)
    xc = cb_ref[...] + cw_ref[CONV_W - 1:CONV_W, :] * xr
    for s in range(1, CONV_W):
        shifted = pltpu.roll(cat, s, axis=0)[SUBLANES:, :]
        xc = xc + cw_ref[CONV_W - 1 - s:CONV_W - s, :] * shifted
    tail_ref[...] = xr[tile - SUBLANES:, :]

    a, b = _lru_coeffs(xc, wa_ref, ba_ref, wi_ref, bi_ref, lam_ref)
    nblocks = tile // SUBLANES
    a = a.reshape(nblocks, SUBLANES, D_RNN)
    b = b.reshape(nblocks, SUBLANES, D_RNN)
    s = 1
    while s < SUBLANES:
        a_prev = _shift_rows_in_blocks(a, s, 1.0)
        b_prev = _shift_rows_in_blocks(b, s, 0.0)
        b = a * b_prev + b
        a = a * a_prev
        s *= 2
    carry = hc_ref[0:1, :]
    blocks = []
    for k in range(nblocks):
        blocks.append(a[k] * carry + b[k])
        carry = a[k, SUBLANES - 1:, :] * carry + b[k, SUBLANES - 1:, :]
    h = jnp.concatenate(blocks, axis=0)
    hc_ref[...] = jnp.broadcast_to(carry, hc_ref.shape)
    hlast_ref[0] = carry
    hg_ref[...] = (h * _gelu(gr_ref[...])).astype(BF16)


def _rnn_prompt(xr, gr, conv_w, conv_b, wa, ba, wi, bi, lam, nb, seq, tile):
    n = xr.shape[0]
    tpr = seq // tile
    tok = pl.BlockSpec((tile, D_RNN), lambda b, t: (b * tpr + t, 0))
    row = pl.BlockSpec((1, D_RNN), lambda b, t: (0, 0))
    sq = pl.BlockSpec((D_RNN, D_RNN), lambda b, t: (0, 0))
    return pl.pallas_call(
        _rnn_prompt_kernel,
        grid=(nb, tpr),
        in_specs=[tok, tok, pl.BlockSpec((CONV_W, D_RNN), lambda b, t: (0, 0)), row, sq, row, sq, row, row],
        out_specs=[tok, pl.BlockSpec((1, 1, D_RNN), lambda b, t: (b, 0, 0))],
        out_shape=[jax.ShapeDtypeStruct((n, D_RNN), BF16), jax.ShapeDtypeStruct((nb, 1, D_RNN), F32)],
        scratch_shapes=[pltpu.VMEM((SUBLANES, D_RNN), F32), pltpu.VMEM((SUBLANES, D_RNN), F32)],
        compiler_params=_params(("arbitrary", "arbitrary"), 48),
    )(xr, gr, conv_w, conv_b, wa, ba, wi, bi, lam)


def _rnn_sample_kernel(xr_ref, gr_ref, c0_ref, c1_ref, c2_ref, h0_ref, cw_ref, cb_ref,
                       wa_ref, ba_ref, wi_ref, bi_ref, lam_ref, hg_ref, h_ref):
    xc = (cb_ref[...] + cw_ref[0:1, :] * c0_ref[...] + cw_ref[1:2, :] * c1_ref[...]
          + cw_ref[2:3, :] * c2_ref[...] + cw_ref[3:4, :] * xr_ref[...])
    a, b = _lru_coeffs(xc, wa_ref, ba_ref, wi_ref, bi_ref, lam_ref)
    h = b + a * h0_ref[...]
    h_ref[...] = h
    hg_ref[...] = (h * _gelu(gr_ref[...])).astype(BF16)


def _rnn_sample(xr, gr, c0, c1, c2, h0, conv_w, conv_b, wa, ba, wi, bi, lam):
    n = xr.shape[0]
    return pl.pallas_call(
        _rnn_sample_kernel,
        out_shape=[jax.ShapeDtypeStruct((n, D_RNN), BF16), jax.ShapeDtypeStruct((n, D_RNN), F32)],
        compiler_params=pltpu.CompilerParams(vmem_limit_bytes=48 * MIB),
    )(xr, gr, c0, c1, c2, h0, conv_w, conv_b, wa, ba, wi, bi, lam)


ATTN_BLOCKS_PER_STEP = 2


def _attn_prompt_kernel(sink_ref, q_ref, kp_ref, kc_ref, vp_ref, vc_ref, o_ref):
    step = pl.program_id(1)
    kk = jnp.concatenate([kp_ref[...], kc_ref[...]], axis=0).astype(BF16)
    vv = jnp.concatenate([vp_ref[...], vc_ref[...]], axis=0).astype(BF16)
    qi = lax.broadcasted_iota(jnp.int32, (WINDOW, 2 * WINDOW), 0)
    ci = lax.broadcasted_iota(jnp.int32, (WINDOW, 2 * WINDOW), 1)
    dist = qi + WINDOW - ci
    band = (dist >= 0) & (dist <= WINDOW)
    for u in range(ATTN_BLOCKS_PER_STEP):
        j = step * ATTN_BLOCKS_PER_STEP + u
        mask = band & (ci + (j - 1) * WINDOW >= 0)
        q = q_ref[pl.ds(u * WINDOW, WINDOW), :]
        outs = []
        for h in range(N_HEADS):
            g = h // GROUP
            qh = q[:, h * HEAD_DIM:(h + 1) * HEAD_DIM]
            kh = kk[u * WINDOW:(u + 2) * WINDOW, g * HEAD_DIM:(g + 1) * HEAD_DIM]
            vh = vv[u * WINDOW:(u + 2) * WINDOW, g * HEAD_DIM:(g + 1) * HEAD_DIM]
            s = lax.dot_general(qh, kh, (((1,), (1,)), ((), ())), preferred_element_type=F32) * (HEAD_DIM ** -0.5)
            s = jnp.where(mask, s, NEG)
            sink = sink_ref[h]
            m = jnp.maximum(jnp.max(s, axis=-1, keepdims=True), sink)
            e = jnp.exp(s - m)
            p = e / (jnp.sum(e, axis=-1, keepdims=True) + jnp.exp(sink - m))
            outs.append(jnp.dot(p.astype(BF16), vh, preferred_element_type=F32))
        o_ref[pl.ds(u * WINDOW, WINDOW), :] = jnp.concatenate(outs, axis=1).astype(BF16)


def _attn_prompt(sinks, q, k, v, nb, seq):
    n = q.shape[0]
    per = ATTN_BLOCKS_PER_STEP
    nstep = seq // (per * WINDOW)
    cur = lambda b, j: (b * nstep + j, 0)
    prev = lambda b, j: (jnp.maximum((b * nstep + j) * per - 1, b * nstep * per), 0)
    return pl.pallas_call(
        _attn_prompt_kernel,
        grid=(nb, nstep),
        in_specs=[pl.BlockSpec(memory_space=pltpu.SMEM),
                  pl.BlockSpec((per * WINDOW, Q_W), cur),
                  pl.BlockSpec((WINDOW, KV_W), prev), pl.BlockSpec((per * WINDOW, KV_W), cur),
                  pl.BlockSpec((WINDOW, KV_W), prev), pl.BlockSpec((per * WINDOW, KV_W), cur)],
        out_specs=pl.BlockSpec((per * WINDOW, Q_W), cur),
        out_shape=jax.ShapeDtypeStruct((n, Q_W), BF16),
        compiler_params=_params(("arbitrary", "arbitrary"), 32),
    )(sinks, q, k, k, v, v)


SAMPLE_ATTN_ROWS = 16


def _attn_sample_kernel(sink_ref, qx_ref, ck_ref, cv_ref, kn_ref, vn_ref, o_ref, wk_ref, wv_ref):
    rows = SAMPLE_ATTN_ROWS
    wbuf = ck_ref.shape[1]
    last = lax.broadcasted_iota(jnp.int32, (wbuf, KV_W), 0) == wbuf - 1
    scale = HEAD_DIM ** -0.5
    for r in range(rows):
        wk_ref[r] = jnp.where(last, kn_ref[r], pltpu.roll(ck_ref[r], wbuf - 1, axis=0))
        wv_ref[r] = jnp.where(last, vn_ref[r], pltpu.roll(cv_ref[r], wbuf - 1, axis=0))
    s = jnp.concatenate(
        [lax.dot_general(qx_ref[r], ck_ref[r].astype(BF16), (((1,), (1,)), ((), ())), preferred_element_type=F32)
         for r in range(rows)], axis=0) * scale
    qx = jnp.concatenate([qx_ref[r] for r in range(rows)], axis=0).astype(F32)
    kn = jnp.concatenate([jnp.broadcast_to(kn_ref[r].astype(BF16).astype(F32), (N_HEADS, KV_W))
                          for r in range(rows)], axis=0)
    sink = jnp.concatenate([sink_ref[...]] * rows, axis=0)
    s_new = jnp.sum(qx * kn, axis=-1, keepdims=True) * scale
    m = jnp.maximum(jnp.maximum(jnp.max(s, axis=-1, keepdims=True), s_new), sink)
    e = jnp.exp(s - m)
    e_new = jnp.exp(s_new - m)
    den = jnp.sum(e, axis=-1, keepdims=True) + e_new + jnp.exp(sink - m)
    p = (e / den).astype(BF16)
    p_new = (e_new / den).astype(BF16).astype(F32)
    for r in range(rows):
        blk = slice(r * N_HEADS, (r + 1) * N_HEADS)
        vn = vn_ref[r].astype(BF16).astype(F32)
        o_ref[r] = jnp.dot(p[blk], cv_ref[r].astype(BF16), preferred_element_type=F32) + p_new[blk] * vn


def _attn_sample(sinks_col, qx, cache_k, cache_v, k_new, v_new):
    nb, wbuf = cache_k.shape[0], cache_k.shape[1]
    rows = SAMPLE_ATTN_ROWS
    blk = lambda d1, d2: pl.BlockSpec((rows, d1, d2), lambda i: (i, 0, 0))
    return pl.pallas_call(
        _attn_sample_kernel,
        grid=(nb // rows,),
        in_specs=[pl.BlockSpec((N_HEADS, 1), lambda i: (0, 0)), blk(N_HEADS, KV_W),
                  blk(wbuf, KV_W), blk(wbuf, KV_W), blk(1, KV_W), blk(1, KV_W)],
        out_specs=[blk(N_HEADS, KV_W), blk(wbuf, KV_W), blk(wbuf, KV_W)],
        out_shape=[jax.ShapeDtypeStruct((nb, N_HEADS, KV_W), F32),
                   jax.ShapeDtypeStruct((nb, wbuf, KV_W), F32), jax.ShapeDtypeStruct((nb, wbuf, KV_W), F32)],
        compiler_params=_params(("arbitrary",), 32),
    )(sinks_col, qx, cache_k, cache_v, k_new, v_new)


def _merge_kernel(hg_ref, o_ref, gt_ref, x_ref, g1_ref, sh_ref, sc_ref, gn_ref, wr_ref, wa_ref, wo_ref,
                  x1_ref, h2t_ref):
    y_r = jnp.dot(hg_ref[...], wr_ref[...], preferred_element_type=F32)
    y_a = jnp.dot(o_ref[...], wa_ref[...], preferred_element_type=F32)
    merged = (jax.nn.sigmoid(gt_ref[:, :D_MODEL]) * y_r + jax.nn.sigmoid(gt_ref[:, D_MODEL:]) * y_a)
    x1 = x_ref[...] + g1_ref[0] * jnp.dot(merged.astype(BF16), wo_ref[...], preferred_element_type=F32)
    x1_ref[...] = x1
    h2 = _rmsnorm(x1, gn_ref[...]) * (1.0 + sc_ref[0]) + sh_ref[0]
    h2t_ref[...] = h2.T.astype(BF16)


def _merge(hg, o, gates, x2d, gt1, sh2, sc2, g2, wr, wa, wo, tile, tiles_per_row):
    n = x2d.shape[0]
    rmod = gt1.shape[1]
    tok = lambda width: pl.BlockSpec((tile, width), lambda i: (i, 0))
    mod = pl.BlockSpec((1, rmod, D_MODEL), lambda i: (i // tiles_per_row, 0, 0))
    full = lambda a: pl.BlockSpec(a.shape, lambda i: (0, 0))
    return pl.pallas_call(
        _merge_kernel,
        grid=(n // tile,),
        in_specs=[tok(D_RNN), tok(Q_W), tok(GATE_W), tok(D_MODEL), mod, mod, mod,
                  pl.BlockSpec((1, D_MODEL), lambda i: (0, 0)), full(wr), full(wa), full(wo)],
        out_specs=[tok(D_MODEL), pl.BlockSpec((D_MODEL, tile), lambda i: (0, i))],
        out_shape=[jax.ShapeDtypeStruct((n, D_MODEL), F32), jax.ShapeDtypeStruct((D_MODEL, n), BF16)],
        compiler_params=_params(("arbitrary",), 48),
    )(hg, o, gates, x2d, gt1, sh2, sc2, g2, wr, wa, wo)


def _vmax(a, b):
    if a is None:
        return b
    if b is None:
        return a
    return jnp.maximum(a, b)


def _vmin(a, b):
    if a is None or b is None:
        return None
    return jnp.minimum(a, b)


def _compare_exchange(xs, i, j):
    hi, lo = _vmax(xs[i], xs[j]), _vmin(xs[i], xs[j])
    xs[i], xs[j] = hi, lo


def _sort_pairs(n):
    pairs = []

    def merge(lo, hi, r):
        step = r * 2
        if step < hi - lo:
            merge(lo, hi, step)
            merge(lo + r, hi, step)
            for i in range(lo + r, hi - r, step):
                pairs.append((i, i + r))
        else:
            pairs.append((lo, lo + r))

    def sort(lo, hi):
        if hi - lo >= 1:
            mid = lo + (hi - lo) // 2
            sort(lo, mid)
            sort(mid + 1, hi)
            merge(lo, hi, 1)

    sort(0, n - 1)
    return pairs


_SORT16 = _sort_pairs(PEER_TOPK)


def _sort_desc(xs):
    xs = list(xs)
    for i, j in _SORT16:
        _compare_exchange(xs, i, j)
    return xs


def _bitonic_to_desc(xs):
    xs = list(xs)
    d = len(xs) // 2
    while d >= 1:
        for i in range(len(xs)):
            if (i & d) == 0:
                _compare_exchange(xs, i, i + d)
        d //= 2
    return xs


def _top_merge(xs, ys):
    k = len(xs)
    return _bitonic_to_desc([_vmax(xs[i], ys[k - 1 - i]) for i in range(k)])


def _top16_rows(s):
    rows = _sort_desc([s[r * SUBLANES:(r + 1) * SUBLANES, :] for r in range(N_KEYS // SUBLANES)])
    shift = SUBLANES // 2
    while shift >= 1:
        rows = _top_merge(rows, [pltpu.roll(x, shift, axis=0) for x in rows])
        shift //= 2
    return rows


def _top16_pair_sums(v1, v2):
    k = PEER_TOPK
    col = lambda j: [v1[i] + v2[j] for i in range(k // (j + 1))]
    g0 = col(0)
    row0_tail = [v1[0] + v2[j] for j in range(k // 2, k)]
    g1 = _bitonic_to_desc(col(1) + row0_tail[::-1])
    g2 = _sort_desc(col(2) + col(3) + col(4) + col(5) + col(6))
    g3 = col(7) + [None] * (k - 2)
    return _top_merge(_top_merge(g0, g1), _top_merge(g2, g3))


PEER_SPECIAL_RANKS = 4


def _threshold_chain(s1_tile, v2_head, tau):
    thr = jnp.full(s1_tile.shape, jnp.inf, F32)
    for v in v2_head:
        thr = jnp.where(s1_tile + v >= tau, v, thr)
    return thr


def _route_kernel(h2t_ref, wq_ref, keys_ref, thr_ref, c1_ref, s2_ref, e2_ref, qt_ref):
    qt_ref[...] = jnp.dot(wq_ref[...], h2t_ref[...], preferred_element_type=F32)
    k = PEER_TOPK
    groups = h2t_ref.shape[1] // LANES
    heads_per_iter = SUBLANES // groups
    sub = lax.broadcasted_iota(jnp.int32, (SUBLANES, LANES), 0)
    generic = k // (PEER_SPECIAL_RANKS + 1)

    def head_group(it, carry):
        packed1, packed2, kept = [None] * k, [None] * k, []
        for hh in range(heads_per_iter):
            h = it * heads_per_iter + hh
            base = pl.multiple_of(h * PEER_DK, PEER_DK)
            q1 = qt_ref[pl.ds(base, PEER_DHALF), :].astype(BF16)
            q2 = qt_ref[pl.ds(base + PEER_DHALF, PEER_DHALF), :].astype(BF16)
            s1 = jnp.dot(keys_ref[h, 0], q1, preferred_element_type=F32)
            s2 = jnp.dot(keys_ref[h, 1], q2, preferred_element_type=F32)
            c1_ref[h] = s1
            s2_ref[h] = s2
            for g in range(groups):
                v1 = _top16_rows(s1[:, g * LANES:(g + 1) * LANES])
                v2 = _top16_rows(s2[:, g * LANES:(g + 1) * LANES])
                slot = hh * groups + g
                for i in range(k):
                    packed1[i] = v1[i] if slot == 0 else jnp.where(sub == slot, v1[i], packed1[i])
                    packed2[i] = v2[i] if slot == 0 else jnp.where(sub == slot, v2[i], packed2[i])
                kept.append((h, g, v1[:PEER_SPECIAL_RANKS], v2[:generic]))

        top = _top16_pair_sums(packed1, packed2)
        tau = top[k - 1]
        z = jnp.ones_like(top[0])
        for f in top[1:]:
            z = z + jnp.exp(f - top[0])
        half_rz = 0.5 / z
        by_rank = [_threshold_chain(packed1[i], packed2[:k // (i + 1)], tau) for i in range(PEER_SPECIAL_RANKS)]

        for slot, (h, g, v1_top, v2_top) in enumerate(kept):
            unpack = lambda x: jnp.broadcast_to(x[slot:slot + 1, :], (SUBLANES, LANES))
            tau_s, rank_s, rz_s = unpack(tau), [unpack(t) for t in by_rank], unpack(half_rz)
            lanes = pl.ds(g * LANES, LANES)
            for r in range(N_KEYS // SUBLANES):
                rows = pl.ds(r * SUBLANES, SUBLANES)
                s1_tile = c1_ref[h, rows, lanes]
                thr = _threshold_chain(s1_tile, v2_top, tau_s)
                for i in reversed(range(PEER_SPECIAL_RANKS)):
                    thr = jnp.where(s1_tile >= v1_top[i], rank_s[i], thr)
                thr_ref[h, rows, lanes] = thr
                c1_ref[h, rows, lanes] = jnp.exp(s1_tile - v1_top[0]) * rz_s
            e2_ref[h, :, lanes] = jnp.exp(s2_ref[h, :, lanes] - v2_top[0][0:1, :])
        return carry

    lax.fori_loop(0, PEER_HEADS // heads_per_iter, head_group, 0)


def _route(h2t, wq_t, keys_bf16, tile):
    n = h2t.shape[1]
    tab = pl.BlockSpec((PEER_HEADS, N_KEYS, tile), lambda i: (0, 0, i))
    tab_shape = jax.ShapeDtypeStruct((PEER_HEADS, N_KEYS, n), F32)
    return pl.pallas_call(
        _route_kernel,
        grid=(n // tile,),
        in_specs=[pl.BlockSpec((D_MODEL, tile), lambda i: (0, i)),
                  pl.BlockSpec(wq_t.shape, lambda i: (0, 0)),
                  pl.BlockSpec(keys_bf16.shape, lambda i: (0, 0, 0, 0))],
        out_specs=[tab, tab, tab, tab],
        out_shape=[tab_shape, tab_shape, tab_shape, tab_shape],
        scratch_shapes=[pltpu.VMEM((PEER_HEADS * PEER_DK, tile), F32)],
        compiler_params=_params(("arbitrary",), 48),
    )(h2t, wq_t, keys_bf16)


def _peer_kernel(h2t_ref, u_ref, vt_ref, thr_ref, c1_ref, s2_ref, e2_ref, x1_ref, g2_ref, gf_ref,
                 y_ref, out_ref, act_ref, wgt_ref):
    j = pl.program_id(1)
    tile = h2t_ref.shape[1]

    @pl.when(j == 0)
    def _():
        out_ref[...] = jnp.zeros_like(out_ref)

    act_ref[...] = jnp.dot(u_ref[...], h2t_ref[...], preferred_element_type=F32)

    def row_group(r, carry):
        r0 = pl.multiple_of(r * PEER_UNIT_ROWS, PEER_UNIT_ROWS)
        rows = pl.ds(r0, PEER_UNIT_ROWS)
        for a in range(PEER_A_PER_BLOCK):
            erows = pl.ds(a * N_KEYS + r0, PEER_UNIT_ROWS)
            gates = [jnp.zeros((PEER_UNIT_ROWS, LANES), F32) for _ in range(tile // LANES)]
            for h in range(PEER_HEADS):
                for g in range(tile // LANES):
                    sl = pl.ds(g * LANES, LANES)
                    sel = jnp.where(s2_ref[h, rows, sl] >= thr_ref[h, a:a + 1, sl], e2_ref[h, rows, sl], 0.0)
                    gates[g] = gates[g] + sel * c1_ref[h, a:a + 1, sl]
            for g in range(tile // LANES):
                sl = pl.ds(g * LANES, LANES)
                x = act_ref[erows, sl]
                wgt_ref[erows, sl] = (gates[g] * (x * (1.0 + lax.erf(x * (2.0 ** -0.5))))).astype(BF16)
        return carry

    lax.fori_loop(0, N_KEYS // PEER_UNIT_ROWS, row_group, 0)
    out_ref[...] += jnp.dot(vt_ref[...], wgt_ref[...], preferred_element_type=F32)

    @pl.when(j == pl.num_programs(1) - 1)
    def _():
        x2 = x1_ref[...] + g2_ref[0] * out_ref[...].T
        y_ref[...] = _rmsnorm(x2, gf_ref[...])


def _peer(h2t, u_bf16, vt_bf16, thr, c1, s2, e2, x1, gt2, g_final, tile, tiles_per_row):
    n = h2t.shape[1]
    eb = PEER_EXPERT_BLOCK
    rmod = gt2.shape[1]
    tab = lambda rows: pl.BlockSpec((PEER_HEADS, rows, tile), lambda i, j: (0, 0, i))
    ablk = pl.BlockSpec((PEER_HEADS, PEER_A_PER_BLOCK, tile), lambda i, j: (0, j, i))
    tok = pl.BlockSpec((tile, D_MODEL), lambda i, j: (i, 0))
    return pl.pallas_call(
        _peer_kernel,
        grid=(n // tile, N_EXPERTS // eb),
        in_specs=[pl.BlockSpec((D_MODEL, tile), lambda i, j: (0, i)),
                  pl.BlockSpec((eb, D_MODEL), lambda i, j: (j, 0)),
                  pl.BlockSpec((D_MODEL, eb), lambda i, j: (0, j)),
                  ablk, ablk, tab(N_KEYS), tab(N_KEYS), tok,
                  pl.BlockSpec((1, rmod, D_MODEL), lambda i, j: (i // tiles_per_row, 0, 0)),
                  pl.BlockSpec((1, D_MODEL), lambda i, j: (0, 0))],
        out_specs=tok,
        out_shape=jax.ShapeDtypeStruct((n, D_MODEL), F32),
        scratch_shapes=[pltpu.VMEM((D_MODEL, tile), F32), pltpu.VMEM((eb, tile), F32),
                        pltpu.VMEM((eb, tile), BF16)],
        compiler_params=_params(("arbitrary", "arbitrary"), 56),
    )(h2t, u_bf16, vt_bf16, thr, c1, s2, e2, x1, gt2, g_final)


def _rope_tables(pos):
    half = HEAD_DIM // 2
    inv = ROPE_THETA ** (-jnp.arange(half, dtype=F32) / half)
    ang = pos.astype(F32)[:, None] * inv[None, :]
    cos, sin = jnp.cos(ang), jnp.sin(ang)
    reps = LANES // HEAD_DIM
    return (jnp.tile(jnp.concatenate([cos, cos], axis=1), (1, reps)),
            jnp.tile(jnp.concatenate([-sin, sin], axis=1), (1, reps)))


def _block_diag(w):
    eye = jnp.eye(RNN_BLOCKS, dtype=w.dtype)
    return (eye[:, None, :, None] * w[:, :, None, :]).reshape(D_RNN, D_RNN)


def _channel_mixer(h2t, x1, gt2, g_final, wq_t, keys, u_bf16, vt_bf16, tile, peer_tile, peer_tiles_per_row):
    thr, c1, s2, e2 = _route(h2t, wq_t, keys, tile)
    return _peer(h2t, u_bf16, vt_bf16, thr, c1, s2, e2, x1, gt2, g_final, peer_tile, peer_tiles_per_row)


def kernel(x_prompt, x_sample, state_conv, state_rnn, cache_win_k, cache_win_v, c_prompt, c_sample, g_norm1, g_norm2, g_final, w_mod, b_mod, w_in, conv_w, conv_b, w_a, b_a, w_i, b_i, lam, w_rnn_out, sinks, w_attn_out, w_out, w_pq, sub_keys, u_tab, v_tab):
    nb, seq, _ = x_prompt.shape
    ns = x_sample.shape[0]
    wbuf = cache_win_k.shape[2]
    assert x_sample.shape[1] == 1 and g_norm1.shape[0] == 1
    assert seq % TOKEN_TILE == 0 and seq % WIDE_TOKEN_TILE == 0 and seq % (ATTN_BLOCKS_PER_STEP * WINDOW) == 0 and seq % WINDOW == 0 and ns % LANES == 0 and seq % PEER_TOKEN_TILE == 0
    l = 0

    w_in_b = w_in[l].astype(BF16)
    wa_d = _block_diag(w_a[l]).astype(BF16)
    wi_d = _block_diag(w_i[l]).astype(BF16)
    wr_b, wat_b, wo_b = w_rnn_out[l].astype(BF16), w_attn_out[l].astype(BF16), w_out[l].astype(BF16)
    wq_t = w_pq[l].T.astype(BF16)
    keys_b = sub_keys[l].astype(BF16)
    u_b = u_tab[l].astype(BF16)
    vt_b = v_tab[l].T.astype(BF16)
    row = lambda v: v.reshape(1, -1)
    g1, g2, gf = row(g_norm1[l]), row(g_norm2[l]), row(g_final)
    cb, ba, bi, lm = row(conv_b[l]), row(b_a[l]), row(b_i[l]), row(lam[l])

    mod = _modulation(jnp.concatenate([c_prompt, c_sample], axis=0), w_mod[l], b_mod[l])
    mod_p = [m.reshape(nb, 1, D_MODEL) for m in jnp.split(mod[:nb], 6, axis=-1)]
    mod_s = [m.reshape(1, ns, D_MODEL) for m in jnp.split(mod[nb:], 6, axis=-1)]

    tile = TOKEN_TILE
    tpr = seq // tile
    xp = x_prompt.reshape(nb * seq, D_MODEL)
    cos_p, sin_p = _rope_tables(jnp.arange(seq))
    wide = WIDE_TOKEN_TILE
    xr, gr, q, k, v, gates = _inproj(xp, mod_p[0], mod_p[1], g1, cos_p, sin_p, w_in_b, wide, seq // wide, seq // wide)
    hg, rnn_p = _rnn_prompt(xr, gr, conv_w[l], cb, wa_d, ba, wi_d, bi, lm, nb, seq, tile)
    o = _attn_prompt(sinks[l], q, k, v, nb, seq)
    x1, h2t = _merge(hg, o, gates, xp, mod_p[2], mod_p[3], mod_p[4], g2, wr_b, wat_b, wo_b, wide, seq // wide)
    y_p = _channel_mixer(h2t, x1, mod_p[5], gf, wq_t, keys_b, u_b, vt_b, wide, PEER_TOKEN_TILE,
                         seq // PEER_TOKEN_TILE)
    y_prompt = y_p.reshape(nb, seq, D_MODEL)
    conv_prompt = xr.reshape(nb, seq, D_RNN)[:, seq - (CONV_W - 1):][None]
    rnn_prompt = rnn_p.reshape(1, nb, D_RNN)
    nbuf = min(WINDOW, seq)
    win_k_prompt = k.reshape(nb, seq, N_KV, HEAD_DIM)[:, seq - nbuf:][None]
    win_v_prompt = v.reshape(nb, seq, N_KV, HEAD_DIM)[:, seq - nbuf:][None]

    xs = x_sample.reshape(ns, D_MODEL)
    cos_s, sin_s = _rope_tables(jnp.full((ns,), PAST_LEN))
    xr_s, gr_s, q_s, k_s, v_s, gates_s = _inproj(xs, mod_s[0], mod_s[1], g1, cos_s, sin_s, w_in_b, ns, 1, 1)
    sc = state_conv[l]
    hg_s, h_s = _rnn_sample(xr_s, gr_s, sc[:, 0], sc[:, 1], sc[:, 2], state_rnn[l],
                            conv_w[l], cb, wa_d, ba, wi_d, bi, lm)
    head_group = (jnp.arange(N_HEADS)[:, None] // GROUP == jnp.arange(N_KV)[None, :])
    qx = jnp.where(head_group[None, :, :, None], q_s.reshape(ns, N_HEADS, 1, HEAD_DIM), 0)
    qx = qx.reshape(ns, N_HEADS, KV_W).astype(BF16)
    ck = cache_win_k[l].reshape(ns, wbuf, KV_W)
    cv = cache_win_v[l].reshape(ns, wbuf, KV_W)
    ox, wk_s, wv_s = _attn_sample(sinks[l].reshape(N_HEADS, 1), qx, ck, cv,
                                  k_s.reshape(ns, 1, KV_W), v_s.reshape(ns, 1, KV_W))
    o_s = jnp.where(head_group[None, :, :, None], ox.reshape(ns, N_HEADS, N_KV, HEAD_DIM), 0.0).sum(axis=2)
    o_s = o_s.reshape(ns, Q_W).astype(BF16)
    x1_s, h2t_s = _merge(hg_s, o_s, gates_s, xs, mod_s[2], mod_s[3], mod_s[4], g2, wr_b, wat_b, wo_b, ns, 1)
    y_s = _channel_mixer(h2t_s, x1_s, mod_s[5], gf, wq_t, keys_b, u_b, vt_b, ns, ns, 1)
    y_sample = y_s.reshape(ns, 1, D_MODEL)
    conv_sample = jnp.concatenate([sc[:, 1:], xr_s[:, None, :]], axis=1)[None]
    rnn_sample = h_s[None]
    win_k_sample = wk_s.reshape(1, ns, wbuf, N_KV, HEAD_DIM)
    win_v_sample = wv_s.reshape(1, ns, wbuf, N_KV, HEAD_DIM)

    return (y_prompt, y_sample, conv_prompt, rnn_prompt, win_k_prompt, win_v_prompt,
            conv_sample, rnn_sample, win_k_sample, win_v_sample)
```

```python
import jax
import jax.numpy as jnp
from jax import lax
from jax.experimental import pallas as pl
from jax.experimental.pallas import tpu as pltpu

F32 = jnp.float32
BF16 = jnp.bfloat16

D_MODEL = 1024
PAST_LEN = 8192
D_RNN = 1280
RNN_BLOCKS = 16
RNN_BW = D_RNN // RNN_BLOCKS
CONV_W = 4
LRU_C = 8.0
N_HEADS = 16
N_KV = 4
HEAD_DIM = 64
GROUP = N_HEADS // N_KV
WINDOW = 128
ROPE_THETA = 10000.0
Q_W = N_HEADS * HEAD_DIM
KV_W = N_KV * HEAD_DIM
PEER_HEADS = 8
N_KEYS = 128
N_EXPERTS = N_KEYS * N_KEYS
PEER_DK = 256
PEER_DHALF = PEER_DK // 2
PEER_TOPK = 16
EPS = 1e-6
NEG = -1e30
GATE_W = 2 * D_MODEL

LANES = 128
SUBLANES = 8
MIB = 1024 * 1024

TOKEN_TILE = 256
WIDE_TOKEN_TILE = 512
PEER_TOKEN_TILE = 512
PEER_UNIT_ROWS = 32
PEER_EXPERT_BLOCK = 2048
PEER_A_PER_BLOCK = PEER_EXPERT_BLOCK // N_KEYS


def _params(semantics, vmem_mib):
    return pltpu.CompilerParams(dimension_semantics=semantics, vmem_limit_bytes=vmem_mib * MIB)


def _gelu(x):
    return 0.5 * x * (1.0 + lax.erf(x * (2.0 ** -0.5)))


def _rmsnorm(x, g):
    return x * lax.rsqrt(jnp.mean(x * x, axis=-1, keepdims=True) + EPS) * g


def _mod_kernel(c_ref, w_ref, b_ref, o_ref):
    c = c_ref[...]
    s = (c * jax.nn.sigmoid(c)).astype(BF16)
    o_ref[...] = jnp.dot(s, w_ref[...].astype(BF16), preferred_element_type=F32) + b_ref[...]


def _modulation(c_all, w_mod, b_mod):
    rows = c_all.shape[0]
    nblk = w_mod.shape[1] // D_MODEL
    return pl.pallas_call(
        _mod_kernel,
        grid=(nblk,),
        in_specs=[pl.BlockSpec((rows, D_MODEL), lambda j: (0, 0)),
                  pl.BlockSpec((D_MODEL, D_MODEL), lambda j: (0, j)),
                  pl.BlockSpec((1, D_MODEL), lambda j: (0, j))],
        out_specs=pl.BlockSpec((rows, D_MODEL), lambda j: (0, j)),
        out_shape=jax.ShapeDtypeStruct((rows, w_mod.shape[1]), F32),
        compiler_params=_params(("arbitrary",), 32),
    )(c_all, w_mod, b_mod.reshape(1, -1))


def _rope(x, cos, sin_signed):
    width = x.shape[1]
    reps = width // LANES
    cosf = jnp.concatenate([cos] * reps, axis=1)
    sinf = jnp.concatenate([sin_signed] * reps, axis=1)
    half = HEAD_DIM // 2
    upper = pltpu.roll(x, width - half, axis=1)
    lower = pltpu.roll(x, half, axis=1)
    lane = lax.broadcasted_iota(jnp.int32, x.shape, 1)
    rot = jnp.where((lane & half) == 0, upper, lower)
    return x * cosf + rot * sinf


def _inproj_kernel(x_ref, sh_ref, sc_ref, g_ref, cos_ref, sin_ref, w_ref,
                   xr_ref, gr_ref, q_ref, k_ref, v_ref, gt_ref):
    x = x_ref[...]
    hn = (_rmsnorm(x, g_ref[...]) * (1.0 + sc_ref[0]) + sh_ref[0]).astype(BF16)

    def proj(c0, width):
        return jnp.dot(hn, w_ref[:, c0:c0 + width], preferred_element_type=F32)

    c = 0
    xr_ref[...] = proj(c, D_RNN); c += D_RNN
    gr_ref[...] = proj(c, D_RNN); c += D_RNN
    q_ref[...] = _rope(proj(c, Q_W), cos_ref[...], sin_ref[...]).astype(BF16); c += Q_W
    k_ref[...] = _rope(proj(c, KV_W), cos_ref[...], sin_ref[...]); c += KV_W
    v_ref[...] = proj(c, KV_W); c += KV_W
    gt_ref[...] = proj(c, GATE_W)


def _inproj(x2d, sh, sc, g, cos, sin_signed, w_in_bf16, tile, tiles_per_row, rope_tiles):
    n = x2d.shape[0]
    rmod = sh.shape[1]
    d_in = w_in_bf16.shape[1]
    tok = lambda width: pl.BlockSpec((tile, width), lambda i: (i, 0))
    mod = pl.BlockSpec((1, rmod, D_MODEL), lambda i: (i // tiles_per_row, 0, 0))
    rope = pl.BlockSpec((tile, LANES), lambda i: (i % rope_tiles, 0))
    return pl.pallas_call(
        _inproj_kernel,
        grid=(n // tile,),
        in_specs=[tok(D_MODEL), mod, mod, pl.BlockSpec((1, D_MODEL), lambda i: (0, 0)), rope, rope,
                  pl.BlockSpec((D_MODEL, d_in), lambda i: (0, 0), pipeline_mode=pl.Buffered(1))],
        out_specs=[tok(D_RNN), tok(D_RNN), tok(Q_W), tok(KV_W), tok(KV_W), tok(GATE_W)],
        out_shape=[jax.ShapeDtypeStruct((n, D_RNN), F32), jax.ShapeDtypeStruct((n, D_RNN), F32),
                   jax.ShapeDtypeStruct((n, Q_W), BF16), jax.ShapeDtypeStruct((n, KV_W), F32),
                   jax.ShapeDtypeStruct((n, KV_W), F32), jax.ShapeDtypeStruct((n, GATE_W), F32)],
        compiler_params=_params(("arbitrary",), 48),
    )(x2d, sh, sc, g, cos, sin_signed, w_in_bf16)


def _log_sigmoid(x):
    return -(jnp.maximum(-x, 0.0) + jnp.log1p(jnp.exp(-jnp.abs(x))))


def _lru_coeffs(xc, wa_ref, ba_ref, wi_ref, bi_ref, lam_ref):
    xb = xc.astype(BF16)
    r = jax.nn.sigmoid(jnp.dot(xb, wa_ref[...], preferred_element_type=F32) + ba_ref[...])
    i = jax.nn.sigmoid(jnp.dot(xb, wi_ref[...], preferred_element_type=F32) + bi_ref[...])
    log_a = LRU_C * r * _log_sigmoid(lam_ref[...])
    a = jnp.exp(log_a)
    m = 1.0 - a * a
    mult = m * lax.rsqrt(jnp.maximum(m, 1e-30))
    return a, mult * (i * xc)


def _shift_rows_in_blocks(x, s, fill):
    rolled = pltpu.roll(x, s, axis=1)
    row = lax.broadcasted_iota(jnp.int32, x.shape, 1)
    return jnp.where(row < s, fill, rolled)


def _rnn_prompt_kernel(xr_ref, gr_ref, cw_ref, cb_ref, wa_ref, ba_ref, wi_ref, bi_ref, lam_ref,
                       hg_ref, hlast_ref, tail_ref, hc_ref):
    t = pl.program_id(1)
    tile = xr_ref.shape[0]

    @pl.when(t == 0)
    def _():
        tail_ref[...] = jnp.zeros_like(tail_ref)
        hc_ref[...] = jnp.zeros_like(hc_ref)

    xr = xr_ref[...]
    cat = jnp.concatenate([tail_ref[...], xr], axis=0)
    xc = cb_ref[...] + cw_ref[CONV_W - 1:CONV_W, :] * xr
    for s in range(1, CONV_W):
        shifted = pltpu.roll(cat, s, axis=0)[SUBLANES:, :]
        xc = xc + cw_ref[CONV_W - 1 - s:CONV_W - s, :] * shifted
    tail_ref[...] = xr[tile - SUBLANES:, :]

    a, b = _lru_coeffs(xc, wa_ref, ba_ref, wi_ref, bi_ref, lam_ref)
    nblocks = tile // SUBLANES
    a = a.reshape(nblocks, SUBLANES, D_RNN)
    b = b.reshape(nblocks, SUBLANES, D_RNN)
    s = 1
    while s < SUBLANES:
        a_prev = _shift_rows_in_blocks(a, s, 1.0)
        b_prev = _shift_rows_in_blocks(b, s, 0.0)
        b = a * b_prev + b
        a = a * a_prev
        s *= 2
    carry = hc_ref[0:1, :]
    blocks = []
    for k in range(nblocks):
        blocks.append(a[k] * carry + b[k])
        carry = a[k, SUBLANES - 1:, :] * carry + b[k, SUBLANES - 1:, :]
    h = jnp.concatenate(blocks, axis=0)
    hc_ref[...] = jnp.broadcast_to(carry, hc_ref.shape)
    hlast_ref[0] = carry
    hg_ref[...] = (h * _gelu(gr_ref[...])).astype(BF16)


def _rnn_prompt(xr, gr, conv_w, conv_b, wa, ba, wi, bi, lam, nb, seq, tile):
    n = xr.shape[0]
    tpr = seq // tile
    tok = pl.BlockSpec((tile, D_RNN), lambda b, t: (b * tpr + t, 0))
    row = pl.BlockSpec((1, D_RNN), lambda b, t: (0, 0))
    sq = pl.BlockSpec((D_RNN, D_RNN), lambda b, t: (0, 0))
    return pl.pallas_call(
        _rnn_prompt_kernel,
        grid=(nb, tpr),
        in_specs=[tok, tok, pl.BlockSpec((CONV_W, D_RNN), lambda b, t: (0, 0)), row, sq, row, sq, row, row],
        out_specs=[tok, pl.BlockSpec((1, 1, D_RNN), lambda b, t: (b, 0, 0))],
        out_shape=[jax.ShapeDtypeStruct((n, D_RNN), BF16), jax.ShapeDtypeStruct((nb, 1, D_RNN), F32)],
        scratch_shapes=[pltpu.VMEM((SUBLANES, D_RNN), F32), pltpu.VMEM((SUBLANES, D_RNN), F32)],
        compiler_params=_params(("arbitrary", "arbitrary"), 48),
    )(xr, gr, conv_w, conv_b, wa, ba, wi, bi, lam)


def _rnn_sample_kernel(xr_ref, gr_ref, c0_ref, c1_ref, c2_ref, h0_ref, cw_ref, cb_ref,
                       wa_ref, ba_ref, wi_ref, bi_ref, lam_ref, hg_ref, h_ref):
    xc = (cb_ref[...] + cw_ref[0:1, :] * c0_ref[...] + cw_ref[1:2, :] * c1_ref[...]
          + cw_ref[2:3, :] * c2_ref[...] + cw_ref[3:4, :] * xr_ref[...])
    a, b = _lru_coeffs(xc, wa_ref, ba_ref, wi_ref, bi_ref, lam_ref)
    h = b + a * h0_ref[...]
    h_ref[...] = h
    hg_ref[...] = (h * _gelu(gr_ref[...])).astype(BF16)


def _rnn_sample(xr, gr, c0, c1, c2, h0, conv_w, conv_b, wa, ba, wi, bi, lam):
    n = xr.shape[0]
    return pl.pallas_call(
        _rnn_sample_kernel,
        out_shape=[jax.ShapeDtypeStruct((n, D_RNN), BF16), jax.ShapeDtypeStruct((n, D_RNN), F32)],
        compiler_params=pltpu.CompilerParams(vmem_limit_bytes=48 * MIB),
    )(xr, gr, c0, c1, c2, h0, conv_w, conv_b, wa, ba, wi, bi, lam)


ATTN_BLOCKS_PER_STEP = 2


def _attn_prompt_kernel(sink_ref, q_ref, kp_ref, kc_ref, vp_ref, vc_ref, o_ref):
    step = pl.program_id(1)
    kk = jnp.concatenate([kp_ref[...], kc_ref[...]], axis=0).astype(BF16)
    vv = jnp.concatenate([vp_ref[...], vc_ref[...]], axis=0).astype(BF16)
    qi = lax.broadcasted_iota(jnp.int32, (WINDOW, 2 * WINDOW), 0)
    ci = lax.broadcasted_iota(jnp.int32, (WINDOW, 2 * WINDOW), 1)
    dist = qi + WINDOW - ci
    band = (dist >= 0) & (dist <= WINDOW)
    for u in range(ATTN_BLOCKS_PER_STEP):
        j = step * ATTN_BLOCKS_PER_STEP + u
        mask = band & (ci + (j - 1) * WINDOW >= 0)
        q = q_ref[pl.ds(u * WINDOW, WINDOW), :]
        outs = []
        for h in range(N_HEADS):
            g = h // GROUP
            qh = q[:, h * HEAD_DIM:(h + 1) * HEAD_DIM]
            kh = kk[u * WINDOW:(u + 2) * WINDOW, g * HEAD_DIM:(g + 1) * HEAD_DIM]
            vh = vv[u * WINDOW:(u + 2) * WINDOW, g * HEAD_DIM:(g + 1) * HEAD_DIM]
            s = lax.dot_general(qh, kh, (((1,), (1,)), ((), ())), preferred_element_type=F32) * (HEAD_DIM ** -0.5)
            s = jnp.where(mask, s, NEG)
            sink = sink_ref[h]
            m = jnp.maximum(jnp.max(s, axis=-1, keepdims=True), sink)
            e = jnp.exp(s - m)
            p = e / (jnp.sum(e, axis=-1, keepdims=True) + jnp.exp(sink - m))
            outs.append(jnp.dot(p.astype(BF16), vh, preferred_element_type=F32))
        o_ref[pl.ds(u * WINDOW, WINDOW), :] = jnp.concatenate(outs, axis=1).astype(BF16)


def _attn_prompt(sinks, q, k, v, nb, seq):
    n = q.shape[0]
    per = ATTN_BLOCKS_PER_STEP
    nstep = seq // (per * WINDOW)
    cur = lambda b, j: (b * nstep + j, 0)
    prev = lambda b, j: (jnp.maximum((b * nstep + j) * per - 1, b * nstep * per), 0)
    return pl.pallas_call(
        _attn_prompt_kernel,
        grid=(nb, nstep),
        in_specs=[pl.BlockSpec(memory_space=pltpu.SMEM),
                  pl.BlockSpec((per * WINDOW, Q_W), cur),
                  pl.BlockSpec((WINDOW, KV_W), prev), pl.BlockSpec((per * WINDOW, KV_W), cur),
                  pl.BlockSpec((WINDOW, KV_W), prev), pl.BlockSpec((per * WINDOW, KV_W), cur)],
        out_specs=pl.BlockSpec((per * WINDOW, Q_W), cur),
        out_shape=jax.ShapeDtypeStruct((n, Q_W), BF16),
        compiler_params=_params(("arbitrary", "arbitrary"), 32),
    )(sinks, q, k, k, v, v)


SAMPLE_ATTN_ROWS = 16


def _attn_sample_kernel(sink_ref, qx_ref, ck_ref, cv_ref, kn_ref, vn_ref, o_ref, wk_ref, wv_ref):
    rows = SAMPLE_ATTN_ROWS
    wbuf = ck_ref.shape[1]
    last = lax.broadcasted_iota(jnp.int32, (wbuf, KV_W), 0) == wbuf - 1
    scale = HEAD_DIM ** -0.5
    for r in range(rows):
        wk_ref[r] = jnp.where(last, kn_ref[r], pltpu.roll(ck_ref[r], wbuf - 1, axis=0))
        wv_ref[r] = jnp.where(last, vn_ref[r], pltpu.roll(cv_ref[r], wbuf - 1, axis=0))
    s = jnp.concatenate(
        [lax.dot_general(qx_ref[r], ck_ref[r].astype(BF16), (((1,), (1,)), ((), ())), preferred_element_type=F32)
         for r in range(rows)], axis=0) * scale
    qx = jnp.concatenate([qx_ref[r] for r in range(rows)], axis=0).astype(F32)
    kn = jnp.concatenate([jnp.broadcast_to(kn_ref[r].astype(BF16).astype(F32), (N_HEADS, KV_W))
                          for r in range(rows)], axis=0)
    sink = jnp.concatenate([sink_ref[...]] * rows, axis=0)
    s_new = jnp.sum(qx * kn, axis=-1, keepdims=True) * scale
    m = jnp.maximum(jnp.maximum(jnp.max(s, axis=-1, keepdims=True), s_new), sink)
    e = jnp.exp(s - m)
    e_new = jnp.exp(s_new - m)
    den = jnp.sum(e, axis=-1, keepdims=True) + e_new + jnp.exp(sink - m)
    p = (e / den).astype(BF16)
    p_new = (e_new / den).astype(BF16).astype(F32)
    for r in range(rows):
        blk = slice(r * N_HEADS, (r + 1) * N_HEADS)
        vn = vn_ref[r].astype(BF16).astype(F32)
        o_ref[r] = jnp.dot(p[blk], cv_ref[r].astype(BF16), preferred_element_type=F32) + p_new[blk] * vn


def _attn_sample(sinks_col, qx, cache_k, cache_v, k_new, v_new):
    nb, wbuf = cache_k.shape[0], cache_k.shape[1]
    rows = SAMPLE_ATTN_ROWS
    blk = lambda d1, d2: pl.BlockSpec((rows, d1, d2), lambda i: (i, 0, 0))
    return pl.pallas_call(
        _attn_sample_kernel,
        grid=(nb // rows,),
        in_specs=[pl.BlockSpec((N_HEADS, 1), lambda i: (0, 0)), blk(N_HEADS, KV_W),
                  blk(wbuf, KV_W), blk(wbuf, KV_W), blk(1, KV_W), blk(1, KV_W)],
        out_specs=[blk(N_HEADS, KV_W), blk(wbuf, KV_W), blk(wbuf, KV_W)],
        out_shape=[jax.ShapeDtypeStruct((nb, N_HEADS, KV_W), F32),
                   jax.ShapeDtypeStruct((nb, wbuf, KV_W), F32), jax.ShapeDtypeStruct((nb, wbuf, KV_W), F32)],
        compiler_params=_params(("arbitrary",), 32),
    )(sinks_col, qx, cache_k, cache_v, k_new, v_new)


def _merge_kernel(hg_ref, o_ref, gt_ref, x_ref, g1_ref, sh_ref, sc_ref, gn_ref, wr_ref, wa_ref, wo_ref,
                  x1_ref, h2t_ref):
    y_r = jnp.dot(hg_ref[...], wr_ref[...], preferred_element_type=F32)
    y_a = jnp.dot(o_ref[...], wa_ref[...], preferred_element_type=F32)
    merged = (jax.nn.sigmoid(gt_ref[:, :D_MODEL]) * y_r + jax.nn.sigmoid(gt_ref[:, D_MODEL:]) * y_a)
    x1 = x_ref[...] + g1_ref[0] * jnp.dot(merged.astype(BF16), wo_ref[...], preferred_element_type=F32)
    x1_ref[...] = x1
    h2 = _rmsnorm(x1, gn_ref[...]) * (1.0 + sc_ref[0]) + sh_ref[0]
    h2t_ref[...] = h2.T.astype(BF16)


def _merge(hg, o, gates, x2d, gt1, sh2, sc2, g2, wr, wa, wo, tile, tiles_per_row):
    n = x2d.shape[0]
    rmod = gt1.shape[1]
    tok = lambda width: pl.BlockSpec((tile, width), lambda i: (i, 0))
    mod = pl.BlockSpec((1, rmod, D_MODEL), lambda i: (i // tiles_per_row, 0, 0))
    full = lambda a: pl.BlockSpec(a.shape, lambda i: (0, 0))
    return pl.pallas_call(
        _merge_kernel,
        grid=(n // tile,),
        in_specs=[tok(D_RNN), tok(Q_W), tok(GATE_W), tok(D_MODEL), mod, mod, mod,
                  pl.BlockSpec((1, D_MODEL), lambda i: (0, 0)), full(wr), full(wa), full(wo)],
        out_specs=[tok(D_MODEL), pl.BlockSpec((D_MODEL, tile), lambda i: (0, i))],
        out_shape=[jax.ShapeDtypeStruct((n, D_MODEL), F32), jax.ShapeDtypeStruct((D_MODEL, n), BF16)],
        compiler_params=_params(("arbitrary",), 48),
    )(hg, o, gates, x2d, gt1, sh2, sc2, g2, wr, wa, wo)


def _vmax(a, b):
    if a is None:
        return b
    if b is None:
        return a
    return jnp.maximum(a, b)


def _vmin(a, b):
    if a is None or b is None:
        return None
    return jnp.minimum(a, b)


def _compare_exchange(xs, i, j):
    hi, lo = _vmax(xs[i], xs[j]), _vmin(xs[i], xs[j])
    xs[i], xs[j] = hi, lo


def _sort_pairs(n):
    pairs = []

    def merge(lo, hi, r):
        step = r * 2
        if step < hi - lo:
            merge(lo, hi, step)
            merge(lo + r, hi, step)
            for i in range(lo + r, hi - r, step):
                pairs.append((i, i + r))
        else:
            pairs.append((lo, lo + r))

    def sort(lo, hi):
        if hi - lo >= 1:
            mid = lo + (hi - lo) // 2
            sort(lo, mid)
            sort(mid + 1, hi)
            merge(lo, hi, 1)

    sort(0, n - 1)
    return pairs


_SORT16 = _sort_pairs(PEER_TOPK)


def _sort_desc(xs):
    xs = list(xs)
    for i, j in _SORT16:
        _compare_exchange(xs, i, j)
    return xs


def _bitonic_to_desc(xs):
    xs = list(xs)
    d = len(xs) // 2
    while d >= 1:
        for i in range(len(xs)):
            if (i & d) == 0:
                _compare_exchange(xs, i, i + d)
        d //= 2
    return xs


def _top_merge(xs, ys):
    k = len(xs)
    return _bitonic_to_desc([_vmax(xs[i], ys[k - 1 - i]) for i in range(k)])


def _top16_rows(s):
    rows = _sort_desc([s[r * SUBLANES:(r + 1) * SUBLANES, :] for r in range(N_KEYS // SUBLANES)])
    shift = SUBLANES // 2
    while shift >= 1:
        rows = _top_merge(rows, [pltpu.roll(x, shift, axis=0) for x in rows])
        shift //= 2
    return rows


def _top16_pair_sums(v1, v2):
    k = PEER_TOPK
    col = lambda j: [v1[i] + v2[j] for i in range(k // (j + 1))]
    g0 = col(0)
    row0_tail = [v1[0] + v2[j] for j in range(k // 2, k)]
    g1 = _bitonic_to_desc(col(1) + row0_tail[::-1])
    g2 = _sort_desc(col(2) + col(3) + col(4) + col(5) + col(6))
    g3 = col(7) + [None] * (k - 2)
    return _top_merge(_top_merge(g0, g1), _top_merge(g2, g3))


PEER_SPECIAL_RANKS = 4


def _threshold_chain(s1_tile, v2_head, tau):
    thr = jnp.full(s1_tile.shape, jnp.inf, F32)
    for v in v2_head:
        thr = jnp.where(s1_tile + v >= tau, v, thr)
    return thr


def _route_kernel(h2t_ref, wq_ref, keys_ref, thr_ref, c1_ref, s2_ref, e2_ref, qt_ref):
    qt_ref[...] = jnp.dot(wq_ref[...], h2t_ref[...], preferred_element_type=F32)
    k = PEER_TOPK
    groups = h2t_ref.shape[1] // LANES
    heads_per_iter = SUBLANES // groups
    sub = lax.broadcasted_iota(jnp.int32, (SUBLANES, LANES), 0)
    generic = k // (PEER_SPECIAL_RANKS + 1)

    def head_group(it, carry):
        packed1, packed2, kept = [None] * k, [None] * k, []
        for hh in range(heads_per_iter):
            h = it * heads_per_iter + hh
            base = pl.multiple_of(h * PEER_DK, PEER_DK)
            q1 = qt_ref[pl.ds(base, PEER_DHALF), :].astype(BF16)
            q2 = qt_ref[pl.ds(base + PEER_DHALF, PEER_DHALF), :].astype(BF16)
            s1 = jnp.dot(keys_ref[h, 0], q1, preferred_element_type=F32)
            s2 = jnp.dot(keys_ref[h, 1], q2, preferred_element_type=F32)
            c1_ref[h] = s1
            s2_ref[h] = s2
            for g in range(groups):
                v1 = _top16_rows(s1[:, g * LANES:(g + 1) * LANES])
                v2 = _top16_rows(s2[:, g * LANES:(g + 1) * LANES])
                slot = hh * groups + g
                for i in range(k):
                    packed1[i] = v1[i] if slot == 0 else jnp.where(sub == slot, v1[i], packed1[i])
                    packed2[i] = v2[i] if slot == 0 else jnp.where(sub == slot, v2[i], packed2[i])
                kept.append((h, g, v1[:PEER_SPECIAL_RANKS], v2[:generic]))

        top = _top16_pair_sums(packed1, packed2)
        tau = top[k - 1]
        z = jnp.ones_like(top[0])
        for f in top[1:]:
            z = z + jnp.exp(f - top[0])
        half_rz = 0.5 / z
        by_rank = [_threshold_chain(packed1[i], packed2[:k // (i + 1)], tau) for i in range(PEER_SPECIAL_RANKS)]

        for slot, (h, g, v1_top, v2_top) in enumerate(kept):
            unpack = lambda x: jnp.broadcast_to(x[slot:slot + 1, :], (SUBLANES, LANES))
            tau_s, rank_s, rz_s = unpack(tau), [unpack(t) for t in by_rank], unpack(half_rz)
            lanes = pl.ds(g * LANES, LANES)
            for r in range(N_KEYS // SUBLANES):
                rows = pl.ds(r * SUBLANES, SUBLANES)
                s1_tile = c1_ref[h, rows, lanes]
                thr = _threshold_chain(s1_tile, v2_top, tau_s)
                for i in reversed(range(PEER_SPECIAL_RANKS)):
                    thr = jnp.where(s1_tile >= v1_top[i], rank_s[i], thr)
                thr_ref[h, rows, lanes] = thr
                c1_ref[h, rows, lanes] = jnp.exp(s1_tile - v1_top[0]) * rz_s
            e2_ref[h, :, lanes] = jnp.exp(s2_ref[h, :, lanes] - v2_top[0][0:1, :])
        return carry

    lax.fori_loop(0, PEER_HEADS // heads_per_iter, head_group, 0)


def _route(h2t, wq_t, keys_bf16, tile):
    n = h2t.shape[1]
    tab = pl.BlockSpec((PEER_HEADS, N_KEYS, tile), lambda i: (0, 0, i))
    tab_shape = jax.ShapeDtypeStruct((PEER_HEADS, N_KEYS, n), F32)
    return pl.pallas_call(
        _route_kernel,
        grid=(n // tile,),
        in_specs=[pl.BlockSpec((D_MODEL, tile), lambda i: (0, i)),
                  pl.BlockSpec(wq_t.shape, lambda i: (0, 0)),
                  pl.BlockSpec(keys_bf16.shape, lambda i: (0, 0, 0, 0))],
        out_specs=[tab, tab, tab, tab],
        out_shape=[tab_shape, tab_shape, tab_shape, tab_shape],
        scratch_shapes=[pltpu.VMEM((PEER_HEADS * PEER_DK, tile), F32)],
        compiler_params=_params(("arbitrary",), 48),
    )(h2t, wq_t, keys_bf16)


def _peer_kernel(h2t_ref, u_ref, vt_ref, thr_ref, c1_ref, s2_ref, e2_ref, x1_ref, g2_ref, gf_ref,
                 y_ref, out_ref, act_ref, wgt_ref):
    j = pl.program_id(1)
    tile = h2t_ref.shape[1]

    @pl.when(j == 0)
    def _():
        out_ref[...] = jnp.zeros_like(out_ref)

    half = PEER_EXPERT_BLOCK // 2
    for r0 in (0, half):
        act_ref[pl.ds(r0, half), :] = jnp.dot(u_ref[pl.ds(r0, half), :], h2t_ref[...], preferred_element_type=F32)

    def row_group(r, carry):
        r0 = pl.multiple_of(r * PEER_UNIT_ROWS, PEER_UNIT_ROWS)
        rows = pl.ds(r0, PEER_UNIT_ROWS)
        for a in range(PEER_A_PER_BLOCK):
            erows = pl.ds(a * N_KEYS + r0, PEER_UNIT_ROWS)
            gates = [jnp.zeros((PEER_UNIT_ROWS, LANES), F32) for _ in range(tile // LANES)]
            for h in range(PEER_HEADS):
                for g in range(tile // LANES):
                    sl = pl.ds(g * LANES, LANES)
                    sel = jnp.where(s2_ref[h, rows, sl] >= thr_ref[h, a:a + 1, sl], e2_ref[h, rows, sl], 0.0)
                    gates[g] = gates[g] + sel * c1_ref[h, a:a + 1, sl]
            for g in range(tile // LANES):
                sl = pl.ds(g * LANES, LANES)
                x = act_ref[erows, sl]
                wgt_ref[erows, sl] = (gates[g] * (x * (1.0 + lax.erf(x * (2.0 ** -0.5))))).astype(BF16)
        return carry

    lax.fori_loop(0, N_KEYS // PEER_UNIT_ROWS, row_group, 0)
    for d0 in (0, D_MODEL // 2):
        out_ref[pl.ds(d0, D_MODEL // 2), :] += jnp.dot(vt_ref[pl.ds(d0, D_MODEL // 2), :], wgt_ref[...],
                                                       preferred_element_type=F32)

    @pl.when(j == pl.num_programs(1) - 1)
    def _():
        x2 = x1_ref[...] + g2_ref[0] * out_ref[...].T
        y_ref[...] = _rmsnorm(x2, gf_ref[...])


def _peer(h2t, u_bf16, vt_bf16, thr, c1, s2, e2, x1, gt2, g_final, tile, tiles_per_row):
    n = h2t.shape[1]
    eb = PEER_EXPERT_BLOCK
    rmod = gt2.shape[1]
    tab = lambda rows: pl.BlockSpec((PEER_HEADS, rows, tile), lambda i, j: (0, 0, i))
    ablk = pl.BlockSpec((PEER_HEADS, PEER_A_PER_BLOCK, tile), lambda i, j: (0, j, i))
    tok = pl.BlockSpec((tile, D_MODEL), lambda i, j: (i, 0))
    return pl.pallas_call(
        _peer_kernel,
        grid=(n // tile, N_EXPERTS // eb),
        in_specs=[pl.BlockSpec((D_MODEL, tile), lambda i, j: (0, i)),
                  pl.BlockSpec((eb, D_MODEL), lambda i, j: (j, 0)),
                  pl.BlockSpec((D_MODEL, eb), lambda i, j: (0, j)),
                  ablk, ablk, tab(N_KEYS), tab(N_KEYS), tok,
                  pl.BlockSpec((1, rmod, D_MODEL), lambda i, j: (i // tiles_per_row, 0, 0)),
                  pl.BlockSpec((1, D_MODEL), lambda i, j: (0, 0))],
        out_specs=tok,
        out_shape=jax.ShapeDtypeStruct((n, D_MODEL), F32),
        scratch_shapes=[pltpu.VMEM((D_MODEL, tile), F32), pltpu.VMEM((eb, tile), F32),
                        pltpu.VMEM((eb, tile), BF16)],
        compiler_params=_params(("arbitrary", "arbitrary"), 56),
    )(h2t, u_bf16, vt_bf16, thr, c1, s2, e2, x1, gt2, g_final)


def _rope_tables(pos):
    half = HEAD_DIM // 2
    inv = ROPE_THETA ** (-jnp.arange(half, dtype=F32) / half)
    ang = pos.astype(F32)[:, None] * inv[None, :]
    cos, sin = jnp.cos(ang), jnp.sin(ang)
    reps = LANES // HEAD_DIM
    return (jnp.tile(jnp.concatenate([cos, cos], axis=1), (1, reps)),
            jnp.tile(jnp.concatenate([-sin, sin], axis=1), (1, reps)))


def _block_diag(w):
    eye = jnp.eye(RNN_BLOCKS, dtype=w.dtype)
    return (eye[:, None, :, None] * w[:, :, None, :]).reshape(D_RNN, D_RNN)


def _channel_mixer(h2t, x1, gt2, g_final, wq_t, keys, u_bf16, vt_bf16, tile, peer_tile, peer_tiles_per_row):
    thr, c1, s2, e2 = _route(h2t, wq_t, keys, tile)
    return _peer(h2t, u_bf16, vt_bf16, thr, c1, s2, e2, x1, gt2, g_final, peer_tile, peer_tiles_per_row)


def kernel(x_prompt, x_sample, state_conv, state_rnn, cache_win_k, cache_win_v, c_prompt, c_sample, g_norm1, g_norm2, g_final, w_mod, b_mod, w_in, conv_w, conv_b, w_a, b_a, w_i, b_i, lam, w_rnn_out, sinks, w_attn_out, w_out, w_pq, sub_keys, u_tab, v_tab):
    nb, seq, _ = x_prompt.shape
    ns = x_sample.shape[0]
    wbuf = cache_win_k.shape[2]
    assert x_sample.shape[1] == 1 and g_norm1.shape[0] == 1
    assert seq % TOKEN_TILE == 0 and seq % WIDE_TOKEN_TILE == 0 and seq % (ATTN_BLOCKS_PER_STEP * WINDOW) == 0 and seq % WINDOW == 0 and ns % LANES == 0 and seq % PEER_TOKEN_TILE == 0
    l = 0

    w_in_b = w_in[l].astype(BF16)
    wa_d = _block_diag(w_a[l]).astype(BF16)
    wi_d = _block_diag(w_i[l]).astype(BF16)
    wr_b, wat_b, wo_b = w_rnn_out[l].astype(BF16), w_attn_out[l].astype(BF16), w_out[l].astype(BF16)
    wq_t = w_pq[l].T.astype(BF16)
    keys_b = sub_keys[l].astype(BF16)
    u_b = u_tab[l].astype(BF16)
    vt_b = v_tab[l].T.astype(BF16)
    row = lambda v: v.reshape(1, -1)
    g1, g2, gf = row(g_norm1[l]), row(g_norm2[l]), row(g_final)
    cb, ba, bi, lm = row(conv_b[l]), row(b_a[l]), row(b_i[l]), row(lam[l])

    mod = _modulation(jnp.concatenate([c_prompt, c_sample], axis=0), w_mod[l], b_mod[l])
    mod_p = [m.reshape(nb, 1, D_MODEL) for m in jnp.split(mod[:nb], 6, axis=-1)]
    mod_s = [m.reshape(1, ns, D_MODEL) for m in jnp.split(mod[nb:], 6, axis=-1)]

    tile = TOKEN_TILE
    tpr = seq // tile
    xp = x_prompt.reshape(nb * seq, D_MODEL)
    cos_p, sin_p = _rope_tables(jnp.arange(seq))
    wide = WIDE_TOKEN_TILE
    xr, gr, q, k, v, gates = _inproj(xp, mod_p[0], mod_p[1], g1, cos_p, sin_p, w_in_b, wide, seq // wide, seq // wide)
    hg, rnn_p = _rnn_prompt(xr, gr, conv_w[l], cb, wa_d, ba, wi_d, bi, lm, nb, seq, tile)
    o = _attn_prompt(sinks[l], q, k, v, nb, seq)
    x1, h2t = _merge(hg, o, gates, xp, mod_p[2], mod_p[3], mod_p[4], g2, wr_b, wat_b, wo_b, wide, seq // wide)
    y_p = _channel_mixer(h2t, x1, mod_p[5], gf, wq_t, keys_b, u_b, vt_b, wide, PEER_TOKEN_TILE,
                         seq // PEER_TOKEN_TILE)
    y_prompt = y_p.reshape(nb, seq, D_MODEL)
    conv_prompt = xr.reshape(nb, seq, D_RNN)[:, seq - (CONV_W - 1):][None]
    rnn_prompt = rnn_p.reshape(1, nb, D_RNN)
    nbuf = min(WINDOW, seq)
    win_k_prompt = k.reshape(nb, seq, N_KV, HEAD_DIM)[:, seq - nbuf:][None]
    win_v_prompt = v.reshape(nb, seq, N_KV, HEAD_DIM)[:, seq - nbuf:][None]

    xs = x_sample.reshape(ns, D_MODEL)
    cos_s, sin_s = _rope_tables(jnp.full((ns,), PAST_LEN))
    xr_s, gr_s, q_s, k_s, v_s, gates_s = _inproj(xs, mod_s[0], mod_s[1], g1, cos_s, sin_s, w_in_b, ns, 1, 1)
    sc = state_conv[l]
    hg_s, h_s = _rnn_sample(xr_s, gr_s, sc[:, 0], sc[:, 1], sc[:, 2], state_rnn[l],
                            conv_w[l], cb, wa_d, ba, wi_d, bi, lm)
    head_group = (jnp.arange(N_HEADS)[:, None] // GROUP == jnp.arange(N_KV)[None, :])
    qx = jnp.where(head_group[None, :, :, None], q_s.reshape(ns, N_HEADS, 1, HEAD_DIM), 0)
    qx = qx.reshape(ns, N_HEADS, KV_W).astype(BF16)
    ck = cache_win_k[l].reshape(ns, wbuf, KV_W)
    cv = cache_win_v[l].reshape(ns, wbuf, KV_W)
    ox, wk_s, wv_s = _attn_sample(sinks[l].reshape(N_HEADS, 1), qx, ck, cv,
                                  k_s.reshape(ns, 1, KV_W), v_s.reshape(ns, 1, KV_W))
    o_s = jnp.where(head_group[None, :, :, None], ox.reshape(ns, N_HEADS, N_KV, HEAD_DIM), 0.0).sum(axis=2)
    o_s = o_s.reshape(ns, Q_W).astype(BF16)
    x1_s, h2t_s = _merge(hg_s, o_s, gates_s, xs, mod_s[2], mod_s[3], mod_s[4], g2, wr_b, wat_b, wo_b, ns, 1)
    y_s = _channel_mixer(h2t_s, x1_s, mod_s[5], gf, wq_t, keys_b, u_b, vt_b, ns, ns, 1)
    y_sample = y_s.reshape(ns, 1, D_MODEL)
    conv_sample = jnp.concatenate([sc[:, 1:], xr_s[:, None, :]], axis=1)[None]
    rnn_sample = h_s[None]
    win_k_sample = wk_s.reshape(1, ns, wbuf, N_KV, HEAD_DIM)
    win_v_sample = wv_s.reshape(1, ns, wbuf, N_KV, HEAD_DIM)

    return (y_prompt, y_sample, conv_prompt, rnn_prompt, win_k_prompt, win_v_prompt,
            conv_sample, rnn_sample, win_k_sample, win_v_sample)
```

```python
import jax
import jax.numpy as jnp
from jax import lax
from jax.experimental import pallas as pl
from jax.experimental.pallas import tpu as pltpu

F32 = jnp.float32
BF16 = jnp.bfloat16

D_MODEL = 1024
PAST_LEN = 8192
D_RNN = 1280
RNN_BLOCKS = 16
RNN_BW = D_RNN // RNN_BLOCKS
CONV_W = 4
LRU_C = 8.0
N_HEADS = 16
N_KV = 4
HEAD_DIM = 64
GROUP = N_HEADS // N_KV
WINDOW = 128
ROPE_THETA = 10000.0
Q_W = N_HEADS * HEAD_DIM
KV_W = N_KV * HEAD_DIM
PEER_HEADS = 8
N_KEYS = 128
N_EXPERTS = N_KEYS * N_KEYS
PEER_DK = 256
PEER_DHALF = PEER_DK // 2
PEER_TOPK = 16
EPS = 1e-6
NEG = -1e30
GATE_W = 2 * D_MODEL

LANES = 128
SUBLANES = 8
MIB = 1024 * 1024

TOKEN_TILE = 256
WIDE_TOKEN_TILE = 512
PEER_TOKEN_TILE = 512
PEER_UNIT_ROWS = 32
PEER_EXPERT_BLOCK = 2048
PEER_A_PER_BLOCK = PEER_EXPERT_BLOCK // N_KEYS


def _params(semantics, vmem_mib):
    return pltpu.CompilerParams(dimension_semantics=semantics, vmem_limit_bytes=vmem_mib * MIB)


def _gelu(x):
    return 0.5 * x * (1.0 + lax.erf(x * (2.0 ** -0.5)))


def _rmsnorm(x, g):
    return x * lax.rsqrt(jnp.mean(x * x, axis=-1, keepdims=True) + EPS) * g


def _mod_kernel(c_ref, w_ref, b_ref, o_ref):
    c = c_ref[...]
    s = (c * jax.nn.sigmoid(c)).astype(BF16)
    o_ref[...] = jnp.dot(s, w_ref[...].astype(BF16), preferred_element_type=F32) + b_ref[...]


def _modulation(c_all, w_mod, b_mod):
    rows = c_all.shape[0]
    nblk = w_mod.shape[1] // D_MODEL
    return pl.pallas_call(
        _mod_kernel,
        grid=(nblk,),
        in_specs=[pl.BlockSpec((rows, D_MODEL), lambda j: (0, 0)),
                  pl.BlockSpec((D_MODEL, D_MODEL), lambda j: (0, j)),
                  pl.BlockSpec((1, D_MODEL), lambda j: (0, j))],
        out_specs=pl.BlockSpec((rows, D_MODEL), lambda j: (0, j)),
        out_shape=jax.ShapeDtypeStruct((rows, w_mod.shape[1]), F32),
        compiler_params=_params(("arbitrary",), 32),
    )(c_all, w_mod, b_mod.reshape(1, -1))


def _rope(x, cos, sin_signed):
    width = x.shape[1]
    reps = width // LANES
    cosf = jnp.concatenate([cos] * reps, axis=1)
    sinf = jnp.concatenate([sin_signed] * reps, axis=1)
    half = HEAD_DIM // 2
    upper = pltpu.roll(x, width - half, axis=1)
    lower = pltpu.roll(x, half, axis=1)
    lane = lax.broadcasted_iota(jnp.int32, x.shape, 1)
    rot = jnp.where((lane & half) == 0, upper, lower)
    return x * cosf + rot * sinf


def _inproj_kernel(x_ref, sh_ref, sc_ref, g_ref, cos_ref, sin_ref, w_ref,
                   xr_ref, gr_ref, q_ref, k_ref, v_ref, gt_ref):
    x = x_ref[...]
    hn = (_rmsnorm(x, g_ref[...]) * (1.0 + sc_ref[0]) + sh_ref[0]).astype(BF16)

    def proj(c0, width):
        return jnp.dot(hn, w_ref[:, c0:c0 + width], preferred_element_type=F32)

    c = 0
    xr_ref[...] = proj(c, D_RNN); c += D_RNN
    gr_ref[...] = proj(c, D_RNN); c += D_RNN
    q_ref[...] = _rope(proj(c, Q_W), cos_ref[...], sin_ref[...]).astype(BF16); c += Q_W
    k_ref[...] = _rope(proj(c, KV_W), cos_ref[...], sin_ref[...]); c += KV_W
    v_ref[...] = proj(c, KV_W); c += KV_W
    gt_ref[...] = proj(c, GATE_W)


def _inproj(x2d, sh, sc, g, cos, sin_signed, w_in_bf16, tile, tiles_per_row, rope_tiles):
    n = x2d.shape[0]
    rmod = sh.shape[1]
    d_in = w_in_bf16.shape[1]
    tok = lambda width: pl.BlockSpec((tile, width), lambda i: (i, 0))
    mod = pl.BlockSpec((1, rmod, D_MODEL), lambda i: (i // tiles_per_row, 0, 0))
    rope = pl.BlockSpec((tile, LANES), lambda i: (i % rope_tiles, 0))
    return pl.pallas_call(
        _inproj_kernel,
        grid=(n // tile,),
        in_specs=[tok(D_MODEL), mod, mod, pl.BlockSpec((1, D_MODEL), lambda i: (0, 0)), rope, rope,
                  pl.BlockSpec((D_MODEL, d_in), lambda i: (0, 0), pipeline_mode=pl.Buffered(1))],
        out_specs=[tok(D_RNN), tok(D_RNN), tok(Q_W), tok(KV_W), tok(KV_W), tok(GATE_W)],
        out_shape=[jax.ShapeDtypeStruct((n, D_RNN), F32), jax.ShapeDtypeStruct((n, D_RNN), F32),
                   jax.ShapeDtypeStruct((n, Q_W), BF16), jax.ShapeDtypeStruct((n, KV_W), F32),
                   jax.ShapeDtypeStruct((n, KV_W), F32), jax.ShapeDtypeStruct((n, GATE_W), F32)],
        compiler_params=_params(("arbitrary",), 48),
    )(x2d, sh, sc, g, cos, sin_signed, w_in_bf16)


def _log_sigmoid(x):
    return -(jnp.maximum(-x, 0.0) + jnp.log1p(jnp.exp(-jnp.abs(x))))


def _lru_coeffs(xc, wa_ref, ba_ref, wi_ref, bi_ref, lam_ref):
    xb = xc.astype(BF16)
    r = jax.nn.sigmoid(jnp.dot(xb, wa_ref[...], preferred_element_type=F32) + ba_ref[...])
    i = jax.nn.sigmoid(jnp.dot(xb, wi_ref[...], preferred_element_type=F32) + bi_ref[...])
    log_a = LRU_C * r * _log_sigmoid(lam_ref[...])
    a = jnp.exp(log_a)
    m = 1.0 - a * a
    mult = m * lax.rsqrt(jnp.maximum(m, 1e-30))
    return a, mult * (i * xc)


def _shift_rows_in_blocks(x, s, fill):
    rolled = pltpu.roll(x, s, axis=1)
    row = lax.broadcasted_iota(jnp.int32, x.shape, 1)
    return jnp.where(row < s, fill, rolled)


def _rnn_prompt_kernel(xr_ref, gr_ref, cw_ref, cb_ref, wa_ref, ba_ref, wi_ref, bi_ref, lam_ref,
                       hg_ref, hlast_ref, tail_ref, hc_ref):
    t = pl.program_id(1)
    tile = xr_ref.shape[0]

    @pl.when(t == 0)
    def _():
        tail_ref[...] = jnp.zeros_like(tail_ref)
        hc_ref[...] = jnp.zeros_like(hc_ref)

    xr = xr_ref[...]
    cat = jnp.concatenate([tail_ref[...], xr], axis=0)
    xc = cb_ref[...] + cw_ref[CONV_W - 1:CONV_W, :] * xr
    for s in range(1, CONV_W):
        shifted = pltpu.roll(cat, s, axis=0)[SUBLANES:, :]
        xc = xc + cw_ref[CONV_W - 1 - s:CONV_W - s, :] * shifted
    tail_ref[...] = xr[tile - SUBLANES:, :]

    a, b = _lru_coeffs(xc, wa_ref, ba_ref, wi_ref, bi_ref, lam_ref)
    nblocks = tile // SUBLANES
    a = a.reshape(nblocks, SUBLANES, D_RNN)
    b = b.reshape(nblocks, SUBLANES, D_RNN)
    s = 1
    while s < SUBLANES:
        a_prev = _shift_rows_in_blocks(a, s, 1.0)
        b_prev = _shift_rows_in_blocks(b, s, 0.0)
        b = a * b_prev + b
        a = a * a_prev
        s *= 2
    carry = hc_ref[0:1, :]
    blocks = []
    for k in range(nblocks):
        blocks.append(a[k] * carry + b[k])
        carry = a[k, SUBLANES - 1:, :] * carry + b[k, SUBLANES - 1:, :]
    h = jnp.concatenate(blocks, axis=0)
    hc_ref[...] = jnp.broadcast_to(carry, hc_ref.shape)
    hlast_ref[0] = carry
    hg_ref[...] = (h * _gelu(gr_ref[...])).astype(BF16)


def _rnn_prompt(xr, gr, conv_w, conv_b, wa, ba, wi, bi, lam, nb, seq, tile):
    n = xr.shape[0]
    tpr = seq // tile
    tok = pl.BlockSpec((tile, D_RNN), lambda b, t: (b * tpr + t, 0))
    row = pl.BlockSpec((1, D_RNN), lambda b, t: (0, 0))
    sq = pl.BlockSpec((D_RNN, D_RNN), lambda b, t: (0, 0))
    return pl.pallas_call(
        _rnn_prompt_kernel,
        grid=(nb, tpr),
        in_specs=[tok, tok, pl.BlockSpec((CONV_W, D_RNN), lambda b, t: (0, 0)), row, sq, row, sq, row, row],
        out_specs=[tok, pl.BlockSpec((1, 1, D_RNN), lambda b, t: (b, 0, 0))],
        out_shape=[jax.ShapeDtypeStruct((n, D_RNN), BF16), jax.ShapeDtypeStruct((nb, 1, D_RNN), F32)],
        scratch_shapes=[pltpu.VMEM((SUBLANES, D_RNN), F32), pltpu.VMEM((SUBLANES, D_RNN), F32)],
        compiler_params=_params(("arbitrary", "arbitrary"), 48),
    )(xr, gr, conv_w, conv_b, wa, ba, wi, bi, lam)


def _rnn_sample_kernel(xr_ref, gr_ref, c0_ref, c1_ref, c2_ref, h0_ref, cw_ref, cb_ref,
                       wa_ref, ba_ref, wi_ref, bi_ref, lam_ref, hg_ref, h_ref):
    xc = (cb_ref[...] + cw_ref[0:1, :] * c0_ref[...] + cw_ref[1:2, :] * c1_ref[...]
          + cw_ref[2:3, :] * c2_ref[...] + cw_ref[3:4, :] * xr_ref[...])
    a, b = _lru_coeffs(xc, wa_ref, ba_ref, wi_ref, bi_ref, lam_ref)
    h = b + a * h0_ref[...]
    h_ref[...] = h
    hg_ref[...] = (h * _gelu(gr_ref[...])).astype(BF16)


def _rnn_sample(xr, gr, c0, c1, c2, h0, conv_w, conv_b, wa, ba, wi, bi, lam):
    n = xr.shape[0]
    return pl.pallas_call(
        _rnn_sample_kernel,
        out_shape=[jax.ShapeDtypeStruct((n, D_RNN), BF16), jax.ShapeDtypeStruct((n, D_RNN), F32)],
        compiler_params=pltpu.CompilerParams(vmem_limit_bytes=48 * MIB),
    )(xr, gr, c0, c1, c2, h0, conv_w, conv_b, wa, ba, wi, bi, lam)


ATTN_BLOCKS_PER_STEP = 2


def _attn_prompt_kernel(sink_ref, q_ref, kp_ref, kc_ref, vp_ref, vc_ref, o_ref):
    step = pl.program_id(1)
    kk = jnp.concatenate([kp_ref[...], kc_ref[...]], axis=0).astype(BF16)
    vv = jnp.concatenate([vp_ref[...], vc_ref[...]], axis=0).astype(BF16)
    qi = lax.broadcasted_iota(jnp.int32, (WINDOW, 2 * WINDOW), 0)
    ci = lax.broadcasted_iota(jnp.int32, (WINDOW, 2 * WINDOW), 1)
    dist = qi + WINDOW - ci
    band = (dist >= 0) & (dist <= WINDOW)
    for u in range(ATTN_BLOCKS_PER_STEP):
        j = step * ATTN_BLOCKS_PER_STEP + u
        mask = band & (ci + (j - 1) * WINDOW >= 0)
        q = q_ref[pl.ds(u * WINDOW, WINDOW), :]
        outs = []
        for h in range(N_HEADS):
            g = h // GROUP
            qh = q[:, h * HEAD_DIM:(h + 1) * HEAD_DIM]
            kh = kk[u * WINDOW:(u + 2) * WINDOW, g * HEAD_DIM:(g + 1) * HEAD_DIM]
            vh = vv[u * WINDOW:(u + 2) * WINDOW, g * HEAD_DIM:(g + 1) * HEAD_DIM]
            s = lax.dot_general(qh, kh, (((1,), (1,)), ((), ())), preferred_element_type=F32) * (HEAD_DIM ** -0.5)
            s = jnp.where(mask, s, NEG)
            sink = sink_ref[h]
            m = jnp.maximum(jnp.max(s, axis=-1, keepdims=True), sink)
            e = jnp.exp(s - m)
            p = e / (jnp.sum(e, axis=-1, keepdims=True) + jnp.exp(sink - m))
            outs.append(jnp.dot(p.astype(BF16), vh, preferred_element_type=F32))
        o_ref[pl.ds(u * WINDOW, WINDOW), :] = jnp.concatenate(outs, axis=1).astype(BF16)


def _attn_prompt(sinks, q, k, v, nb, seq):
    n = q.shape[0]
    per = ATTN_BLOCKS_PER_STEP
    nstep = seq // (per * WINDOW)
    cur = lambda b, j: (b * nstep + j, 0)
    prev = lambda b, j: (jnp.maximum((b * nstep + j) * per - 1, b * nstep * per), 0)
    return pl.pallas_call(
        _attn_prompt_kernel,
        grid=(nb, nstep),
        in_specs=[pl.BlockSpec(memory_space=pltpu.SMEM),
                  pl.BlockSpec((per * WINDOW, Q_W), cur),
                  pl.BlockSpec((WINDOW, KV_W), prev), pl.BlockSpec((per * WINDOW, KV_W), cur),
                  pl.BlockSpec((WINDOW, KV_W), prev), pl.BlockSpec((per * WINDOW, KV_W), cur)],
        out_specs=pl.BlockSpec((per * WINDOW, Q_W), cur),
        out_shape=jax.ShapeDtypeStruct((n, Q_W), BF16),
        compiler_params=_params(("arbitrary", "arbitrary"), 32),
    )(sinks, q, k, k, v, v)


SAMPLE_ATTN_ROWS = 16


def _attn_sample_kernel(sink_ref, qx_ref, ck_ref, cv_ref, kn_ref, vn_ref, o_ref, wk_ref, wv_ref):
    rows = SAMPLE_ATTN_ROWS
    wbuf = ck_ref.shape[1]
    last = lax.broadcasted_iota(jnp.int32, (wbuf, KV_W), 0) == wbuf - 1
    scale = HEAD_DIM ** -0.5
    for r in range(rows):
        wk_ref[r] = jnp.where(last, kn_ref[r], pltpu.roll(ck_ref[r], wbuf - 1, axis=0))
        wv_ref[r] = jnp.where(last, vn_ref[r], pltpu.roll(cv_ref[r], wbuf - 1, axis=0))
    s = jnp.concatenate(
        [lax.dot_general(qx_ref[r], ck_ref[r].astype(BF16), (((1,), (1,)), ((), ())), preferred_element_type=F32)
         for r in range(rows)], axis=0) * scale
    qx = jnp.concatenate([qx_ref[r] for r in range(rows)], axis=0).astype(F32)
    kn = jnp.concatenate([jnp.broadcast_to(kn_ref[r].astype(BF16).astype(F32), (N_HEADS, KV_W))
                          for r in range(rows)], axis=0)
    sink = jnp.concatenate([sink_ref[...]] * rows, axis=0)
    s_new = jnp.sum(qx * kn, axis=-1, keepdims=True) * scale
    m = jnp.maximum(jnp.maximum(jnp.max(s, axis=-1, keepdims=True), s_new), sink)
    e = jnp.exp(s - m)
    e_new = jnp.exp(s_new - m)
    den = jnp.sum(e, axis=-1, keepdims=True) + e_new + jnp.exp(sink - m)
    p = (e / den).astype(BF16)
    p_new = (e_new / den).astype(BF16).astype(F32)
    for r in range(rows):
        blk = slice(r * N_HEADS, (r + 1) * N_HEADS)
        vn = vn_ref[r].astype(BF16).astype(F32)
        o_ref[r] = jnp.dot(p[blk], cv_ref[r].astype(BF16), preferred_element_type=F32) + p_new[blk] * vn


def _attn_sample(sinks_col, qx, cache_k, cache_v, k_new, v_new):
    nb, wbuf = cache_k.shape[0], cache_k.shape[1]
    rows = SAMPLE_ATTN_ROWS
    blk = lambda d1, d2: pl.BlockSpec((rows, d1, d2), lambda i: (i, 0, 0))
    return pl.pallas_call(
        _attn_sample_kernel,
        grid=(nb // rows,),
        in_specs=[pl.BlockSpec((N_HEADS, 1), lambda i: (0, 0)), blk(N_HEADS, KV_W),
                  blk(wbuf, KV_W), blk(wbuf, KV_W), blk(1, KV_W), blk(1, KV_W)],
        out_specs=[blk(N_HEADS, KV_W), blk(wbuf, KV_W), blk(wbuf, KV_W)],
        out_shape=[jax.ShapeDtypeStruct((nb, N_HEADS, KV_W), F32),
                   jax.ShapeDtypeStruct((nb, wbuf, KV_W), F32), jax.ShapeDtypeStruct((nb, wbuf, KV_W), F32)],
        compiler_params=_params(("arbitrary",), 32),
    )(sinks_col, qx, cache_k, cache_v, k_new, v_new)


def _merge_kernel(hg_ref, o_ref, gt_ref, x_ref, g1_ref, sh_ref, sc_ref, gn_ref, wr_ref, wa_ref, wo_ref,
                  x1_ref, h2t_ref):
    y_r = jnp.dot(hg_ref[...], wr_ref[...], preferred_element_type=F32)
    y_a = jnp.dot(o_ref[...], wa_ref[...], preferred_element_type=F32)
    merged = (jax.nn.sigmoid(gt_ref[:, :D_MODEL]) * y_r + jax.nn.sigmoid(gt_ref[:, D_MODEL:]) * y_a)
    x1 = x_ref[...] + g1_ref[0] * jnp.dot(merged.astype(BF16), wo_ref[...], preferred_element_type=F32)
    x1_ref[...] = x1
    h2 = _rmsnorm(x1, gn_ref[...]) * (1.0 + sc_ref[0]) + sh_ref[0]
    h2t_ref[...] = h2.T.astype(BF16)


def _merge(hg, o, gates, x2d, gt1, sh2, sc2, g2, wr, wa, wo, tile, tiles_per_row):
    n = x2d.shape[0]
    rmod = gt1.shape[1]
    tok = lambda width: pl.BlockSpec((tile, width), lambda i: (i, 0))
    mod = pl.BlockSpec((1, rmod, D_MODEL), lambda i: (i // tiles_per_row, 0, 0))
    full = lambda a: pl.BlockSpec(a.shape, lambda i: (0, 0))
    return pl.pallas_call(
        _merge_kernel,
        grid=(n // tile,),
        in_specs=[tok(D_RNN), tok(Q_W), tok(GATE_W), tok(D_MODEL), mod, mod, mod,
                  pl.BlockSpec((1, D_MODEL), lambda i: (0, 0)), full(wr), full(wa), full(wo)],
        out_specs=[tok(D_MODEL), pl.BlockSpec((D_MODEL, tile), lambda i: (0, i))],
        out_shape=[jax.ShapeDtypeStruct((n, D_MODEL), F32), jax.ShapeDtypeStruct((D_MODEL, n), BF16)],
        compiler_params=_params(("arbitrary",), 48),
    )(hg, o, gates, x2d, gt1, sh2, sc2, g2, wr, wa, wo)


def _vmax(a, b):
    if a is None:
        return b
    if b is None:
        return a
    return jnp.maximum(a, b)


def _vmin(a, b):
    if a is None or b is None:
        return None
    return jnp.minimum(a, b)


def _compare_exchange(xs, i, j):
    hi, lo = _vmax(xs[i], xs[j]), _vmin(xs[i], xs[j])
    xs[i], xs[j] = hi, lo


def _sort_pairs(n):
    pairs = []

    def merge(lo, hi, r):
        step = r * 2
        if step < hi - lo:
            merge(lo, hi, step)
            merge(lo + r, hi, step)
            for i in range(lo + r, hi - r, step):
                pairs.append((i, i + r))
        else:
            pairs.append((lo, lo + r))

    def sort(lo, hi):
        if hi - lo >= 1:
            mid = lo + (hi - lo) // 2
            sort(lo, mid)
            sort(mid + 1, hi)
            merge(lo, hi, 1)

    sort(0, n - 1)
    return pairs


_SORT16 = _sort_pairs(PEER_TOPK)


def _sort_desc(xs):
    xs = list(xs)
    for i, j in _SORT16:
        _compare_exchange(xs, i, j)
    return xs


def _bitonic_to_desc(xs):
    xs = list(xs)
    d = len(xs) // 2
    while d >= 1:
        for i in range(len(xs)):
            if (i & d) == 0:
                _compare_exchange(xs, i, i + d)
        d //= 2
    return xs


def _top_merge(xs, ys):
    k = len(xs)
    return _bitonic_to_desc([_vmax(xs[i], ys[k - 1 - i]) for i in range(k)])


def _top16_rows(s):
    rows = _sort_desc([s[r * SUBLANES:(r + 1) * SUBLANES, :] for r in range(N_KEYS // SUBLANES)])
    shift = SUBLANES // 2
    while shift >= 1:
        rows = _top_merge(rows, [pltpu.roll(x, shift, axis=0) for x in rows])
        shift //= 2
    return rows


def _top16_pair_sums(v1, v2):
    k = PEER_TOPK
    col = lambda j: [v1[i] + v2[j] for i in range(k // (j + 1))]
    g0 = col(0)
    row0_tail = [v1[0] + v2[j] for j in range(k // 2, k)]
    g1 = _bitonic_to_desc(col(1) + row0_tail[::-1])
    g2 = _sort_desc(col(2) + col(3) + col(4) + col(5) + col(6))
    g3 = col(7) + [None] * (k - 2)
    return _top_merge(_top_merge(g0, g1), _top_merge(g2, g3))


PEER_SPECIAL_RANKS = 4


def _threshold_chain(s1_tile, v2_head, tau):
    thr = jnp.full(s1_tile.shape, jnp.inf, F32)
    for v in v2_head:
        thr = jnp.where(s1_tile + v >= tau, v, thr)
    return thr


def _route_kernel(h2t_ref, wq_ref, keys_ref, thr_ref, c1_ref, s2_ref, e2_ref, qt_ref):
    qt_ref[...] = jnp.dot(wq_ref[...], h2t_ref[...], preferred_element_type=F32)
    k = PEER_TOPK
    groups = h2t_ref.shape[1] // LANES
    heads_per_iter = SUBLANES // groups
    sub = lax.broadcasted_iota(jnp.int32, (SUBLANES, LANES), 0)
    generic = k // (PEER_SPECIAL_RANKS + 1)

    def head_group(it, carry):
        packed1, packed2, kept = [None] * k, [None] * k, []
        for hh in range(heads_per_iter):
            h = it * heads_per_iter + hh
            base = pl.multiple_of(h * PEER_DK, PEER_DK)
            q1 = qt_ref[pl.ds(base, PEER_DHALF), :].astype(BF16)
            q2 = qt_ref[pl.ds(base + PEER_DHALF, PEER_DHALF), :].astype(BF16)
            s1 = jnp.dot(keys_ref[h, 0], q1, preferred_element_type=F32)
            s2 = jnp.dot(keys_ref[h, 1], q2, preferred_element_type=F32)
            c1_ref[h] = s1
            s2_ref[h] = s2
            for g in range(groups):
                v1 = _top16_rows(s1[:, g * LANES:(g + 1) * LANES])
                v2 = _top16_rows(s2[:, g * LANES:(g + 1) * LANES])
                slot = hh * groups + g
                for i in range(k):
                    packed1[i] = v1[i] if slot == 0 else jnp.where(sub == slot, v1[i], packed1[i])
                    packed2[i] = v2[i] if slot == 0 else jnp.where(sub == slot, v2[i], packed2[i])
                kept.append((h, g, v1[:PEER_SPECIAL_RANKS], v2[:generic]))

        top = _top16_pair_sums(packed1, packed2)
        tau = top[k - 1]
        z = jnp.ones_like(top[0])
        for f in top[1:]:
            z = z + jnp.exp(f - top[0])
        half_rz = 0.5 / z
        by_rank = [_threshold_chain(packed1[i], packed2[:k // (i + 1)], tau) for i in range(PEER_SPECIAL_RANKS)]

        for slot, (h, g, v1_top, v2_top) in enumerate(kept):
            unpack = lambda x: jnp.broadcast_to(x[slot:slot + 1, :], (SUBLANES, LANES))
            tau_s, rank_s, rz_s = unpack(tau), [unpack(t) for t in by_rank], unpack(half_rz)
            lanes = pl.ds(g * LANES, LANES)
            for r in range(N_KEYS // SUBLANES):
                rows = pl.ds(r * SUBLANES, SUBLANES)
                s1_tile = c1_ref[h, rows, lanes]
                thr = _threshold_chain(s1_tile, v2_top, tau_s)
                for i in reversed(range(PEER_SPECIAL_RANKS)):
                    thr = jnp.where(s1_tile >= v1_top[i], rank_s[i], thr)
                thr_ref[h, rows, lanes] = thr
                c1_ref[h, rows, lanes] = jnp.exp(s1_tile - v1_top[0]) * rz_s
            e2_ref[h, :, lanes] = jnp.exp(s2_ref[h, :, lanes] - v2_top[0][0:1, :])
        return carry

    lax.fori_loop(0, PEER_HEADS // heads_per_iter, head_group, 0)


def _route(h2t, wq_t, keys_bf16, tile):
    n = h2t.shape[1]
    tab = pl.BlockSpec((PEER_HEADS, N_KEYS, tile), lambda i: (0, 0, i))
    tab_shape = jax.ShapeDtypeStruct((PEER_HEADS, N_KEYS, n), F32)
    return pl.pallas_call(
        _route_kernel,
        grid=(n // tile,),
        in_specs=[pl.BlockSpec((D_MODEL, tile), lambda i: (0, i)),
                  pl.BlockSpec(wq_t.shape, lambda i: (0, 0)),
                  pl.BlockSpec(keys_bf16.shape, lambda i: (0, 0, 0, 0))],
        out_specs=[tab, tab, tab, tab],
        out_shape=[tab_shape, tab_shape, tab_shape, tab_shape],
        scratch_shapes=[pltpu.VMEM((PEER_HEADS * PEER_DK, tile), F32)],
        compiler_params=_params(("arbitrary",), 48),
    )(h2t, wq_t, keys_bf16)


def _peer_kernel(h2t_ref, u_ref, vt_ref, thr_ref, c1_ref, s2_ref, e2_ref, x1_ref, g2_ref, gf_ref,
                 y_ref, out_ref, act_ref, wgt_ref):
    j = pl.program_id(1)
    tile = h2t_ref.shape[1]

    @pl.when(j == 0)
    def _():
        out_ref[...] = jnp.zeros_like(out_ref)

    half = PEER_EXPERT_BLOCK // 2
    for r0 in (0, half):
        act_ref[pl.ds(r0, half), :] = jnp.dot(u_ref[pl.ds(r0, half), :], h2t_ref[...], preferred_element_type=F32)

    def row_group(r, carry):
        r0 = pl.multiple_of(r * PEER_UNIT_ROWS, PEER_UNIT_ROWS)
        rows = pl.ds(r0, PEER_UNIT_ROWS)
        for a0 in range(0, PEER_A_PER_BLOCK, 2):
            gates = [[jnp.zeros((PEER_UNIT_ROWS, LANES), F32) for _ in range(tile // LANES)] for _ in range(2)]
            for h in range(PEER_HEADS):
                for g in range(tile // LANES):
                    sl = pl.ds(g * LANES, LANES)
                    s2 = s2_ref[h, rows, sl]
                    e2 = e2_ref[h, rows, sl]
                    for i in range(2):
                        a = a0 + i
                        sel = jnp.where(s2 >= thr_ref[h, a:a + 1, sl], e2, 0.0)
                        gates[i][g] = gates[i][g] + sel * c1_ref[h, a:a + 1, sl]
            for i in range(2):
                erows = pl.ds((a0 + i) * N_KEYS + r0, PEER_UNIT_ROWS)
                for g in range(tile // LANES):
                    sl = pl.ds(g * LANES, LANES)
                    x = act_ref[erows, sl]
                    wgt_ref[erows, sl] = (gates[i][g] * (x * (1.0 + lax.erf(x * (2.0 ** -0.5))))).astype(BF16)
        return carry

    lax.fori_loop(0, N_KEYS // PEER_UNIT_ROWS, row_group, 0)
    for d0 in (0, D_MODEL // 2):
        out_ref[pl.ds(d0, D_MODEL // 2), :] += jnp.dot(vt_ref[pl.ds(d0, D_MODEL // 2), :], wgt_ref[...],
                                                       preferred_element_type=F32)

    @pl.when(j == pl.num_programs(1) - 1)
    def _():
        x2 = x1_ref[...] + g2_ref[0] * out_ref[...].T
        y_ref[...] = _rmsnorm(x2, gf_ref[...])


def _peer(h2t, u_bf16, vt_bf16, thr, c1, s2, e2, x1, gt2, g_final, tile, tiles_per_row):
    n = h2t.shape[1]
    eb = PEER_EXPERT_BLOCK
    rmod = gt2.shape[1]
    tab = lambda rows: pl.BlockSpec((PEER_HEADS, rows, tile), lambda i, j: (0, 0, i))
    ablk = pl.BlockSpec((PEER_HEADS, PEER_A_PER_BLOCK, tile), lambda i, j: (0, j, i))
    tok = pl.BlockSpec((tile, D_MODEL), lambda i, j: (i, 0))
    return pl.pallas_call(
        _peer_kernel,
        grid=(n // tile, N_EXPERTS // eb),
        in_specs=[pl.BlockSpec((D_MODEL, tile), lambda i, j: (0, i)),
                  pl.BlockSpec((eb, D_MODEL), lambda i, j: (j, 0)),
                  pl.BlockSpec((D_MODEL, eb), lambda i, j: (0, j)),
                  ablk, ablk, tab(N_KEYS), tab(N_KEYS), tok,
                  pl.BlockSpec((1, rmod, D_MODEL), lambda i, j: (i // tiles_per_row, 0, 0)),
                  pl.BlockSpec((1, D_MODEL), lambda i, j: (0, 0))],
        out_specs=tok,
        out_shape=jax.ShapeDtypeStruct((n, D_MODEL), F32),
        scratch_shapes=[pltpu.VMEM((D_MODEL, tile), F32), pltpu.VMEM((eb, tile), F32),
                        pltpu.VMEM((eb, tile), BF16)],
        compiler_params=_params(("arbitrary", "arbitrary"), 56),
    )(h2t, u_bf16, vt_bf16, thr, c1, s2, e2, x1, gt2, g_final)


def _rope_tables(pos):
    half = HEAD_DIM // 2
    inv = ROPE_THETA ** (-jnp.arange(half, dtype=F32) / half)
    ang = pos.astype(F32)[:, None] * inv[None, :]
    cos, sin = jnp.cos(ang), jnp.sin(ang)
    reps = LANES // HEAD_DIM
    return (jnp.tile(jnp.concatenate([cos, cos], axis=1), (1, reps)),
            jnp.tile(jnp.concatenate([-sin, sin], axis=1), (1, reps)))


def _block_diag(w):
    eye = jnp.eye(RNN_BLOCKS, dtype=w.dtype)
    return (eye[:, None, :, None] * w[:, :, None, :]).reshape(D_RNN, D_RNN)


def _channel_mixer(h2t, x1, gt2, g_final, wq_t, keys, u_bf16, vt_bf16, tile, peer_tile, peer_tiles_per_row):
    thr, c1, s2, e2 = _route(h2t, wq_t, keys, tile)
    return _peer(h2t, u_bf16, vt_bf16, thr, c1, s2, e2, x1, gt2, g_final, peer_tile, peer_tiles_per_row)


def kernel(x_prompt, x_sample, state_conv, state_rnn, cache_win_k, cache_win_v, c_prompt, c_sample, g_norm1, g_norm2, g_final, w_mod, b_mod, w_in, conv_w, conv_b, w_a, b_a, w_i, b_i, lam, w_rnn_out, sinks, w_attn_out, w_out, w_pq, sub_keys, u_tab, v_tab):
    nb, seq, _ = x_prompt.shape
    ns = x_sample.shape[0]
    wbuf = cache_win_k.shape[2]
    assert x_sample.shape[1] == 1 and g_norm1.shape[0] == 1
    assert seq % TOKEN_TILE == 0 and seq % WIDE_TOKEN_TILE == 0 and seq % (ATTN_BLOCKS_PER_STEP * WINDOW) == 0 and seq % WINDOW == 0 and ns % LANES == 0 and seq % PEER_TOKEN_TILE == 0
    l = 0

    w_in_b = w_in[l].astype(BF16)
    wa_d = _block_diag(w_a[l]).astype(BF16)
    wi_d = _block_diag(w_i[l]).astype(BF16)
    wr_b, wat_b, wo_b = w_rnn_out[l].astype(BF16), w_attn_out[l].astype(BF16), w_out[l].astype(BF16)
    wq_t = w_pq[l].T.astype(BF16)
    keys_b = sub_keys[l].astype(BF16)
    u_b = u_tab[l].astype(BF16)
    vt_b = v_tab[l].T.astype(BF16)
    row = lambda v: v.reshape(1, -1)
    g1, g2, gf = row(g_norm1[l]), row(g_norm2[l]), row(g_final)
    cb, ba, bi, lm = row(conv_b[l]), row(b_a[l]), row(b_i[l]), row(lam[l])

    mod = _modulation(jnp.concatenate([c_prompt, c_sample], axis=0), w_mod[l], b_mod[l])
    mod_p = [m.reshape(nb, 1, D_MODEL) for m in jnp.split(mod[:nb], 6, axis=-1)]
    mod_s = [m.reshape(1, ns, D_MODEL) for m in jnp.split(mod[nb:], 6, axis=-1)]

    tile = TOKEN_TILE
    tpr = seq // tile
    xp = x_prompt.reshape(nb * seq, D_MODEL)
    cos_p, sin_p = _rope_tables(jnp.arange(seq))
    wide = WIDE_TOKEN_TILE
    xr, gr, q, k, v, gates = _inproj(xp, mod_p[0], mod_p[1], g1, cos_p, sin_p, w_in_b, wide, seq // wide, seq // wide)
    hg, rnn_p = _rnn_prompt(xr, gr, conv_w[l], cb, wa_d, ba, wi_d, bi, lm, nb, seq, tile)
    o = _attn_prompt(sinks[l], q, k, v, nb, seq)
    x1, h2t = _merge(hg, o, gates, xp, mod_p[2], mod_p[3], mod_p[4], g2, wr_b, wat_b, wo_b, wide, seq // wide)
    y_p = _channel_mixer(h2t, x1, mod_p[5], gf, wq_t, keys_b, u_b, vt_b, wide, PEER_TOKEN_TILE,
                         seq // PEER_TOKEN_TILE)
    y_prompt = y_p.reshape(nb, seq, D_MODEL)
    conv_prompt = xr.reshape(nb, seq, D_RNN)[:, seq - (CONV_W - 1):][None]
    rnn_prompt = rnn_p.reshape(1, nb, D_RNN)
    nbuf = min(WINDOW, seq)
    win_k_prompt = k.reshape(nb, seq, N_KV, HEAD_DIM)[:, seq - nbuf:][None]
    win_v_prompt = v.reshape(nb, seq, N_KV, HEAD_DIM)[:, seq - nbuf:][None]

    xs = x_sample.reshape(ns, D_MODEL)
    cos_s, sin_s = _rope_tables(jnp.full((ns,), PAST_LEN))
    xr_s, gr_s, q_s, k_s, v_s, gates_s = _inproj(xs, mod_s[0], mod_s[1], g1, cos_s, sin_s, w_in_b, ns, 1, 1)
    sc = state_conv[l]
    hg_s, h_s = _rnn_sample(xr_s, gr_s, sc[:, 0], sc[:, 1], sc[:, 2], state_rnn[l],
                            conv_w[l], cb, wa_d, ba, wi_d, bi, lm)
    head_group = (jnp.arange(N_HEADS)[:, None] // GROUP == jnp.arange(N_KV)[None, :])
    qx = jnp.where(head_group[None, :, :, None], q_s.reshape(ns, N_HEADS, 1, HEAD_DIM), 0)
    qx = qx.reshape(ns, N_HEADS, KV_W).astype(BF16)
    ck = cache_win_k[l].reshape(ns, wbuf, KV_W)
    cv = cache_win_v[l].reshape(ns, wbuf, KV_W)
    ox, wk_s, wv_s = _attn_sample(sinks[l].reshape(N_HEADS, 1), qx, ck, cv,
                                  k_s.reshape(ns, 1, KV_W), v_s.reshape(ns, 1, KV_W))
    o_s = jnp.where(head_group[None, :, :, None], ox.reshape(ns, N_HEADS, N_KV, HEAD_DIM), 0.0).sum(axis=2)
    o_s = o_s.reshape(ns, Q_W).astype(BF16)
    x1_s, h2t_s = _merge(hg_s, o_s, gates_s, xs, mod_s[2], mod_s[3], mod_s[4], g2, wr_b, wat_b, wo_b, ns, 1)
    y_s = _channel_mixer(h2t_s, x1_s, mod_s[5], gf, wq_t, keys_b, u_b, vt_b, ns, ns, 1)
    y_sample = y_s.reshape(ns, 1, D_MODEL)
    conv_sample = jnp.concatenate([sc[:, 1:], xr_s[:, None, :]], axis=1)[None]
    rnn_sample = h_s[None]
    win_k_sample = wk_s.reshape(1, ns, wbuf, N_KV, HEAD_DIM)
    win_v_sample = wv_s.reshape(1, ns, wbuf, N_KV, HEAD_DIM)

    return (y_prompt, y_sample, conv_prompt, rnn_prompt, win_k_prompt, win_v_prompt,
            conv_sample, rnn_sample, win_k_sample, win_v_sample)
```

```python
import jax
import jax.numpy as jnp
from jax import lax
from jax.experimental import pallas as pl
from jax.experimental.pallas import tpu as pltpu

F32 = jnp.float32
BF16 = jnp.bfloat16

D_MODEL = 1024
PAST_LEN = 8192
D_RNN = 1280
RNN_BLOCKS = 16
RNN_BW = D_RNN // RNN_BLOCKS
CONV_W = 4
LRU_C = 8.0
N_HEADS = 16
N_KV = 4
HEAD_DIM = 64
GROUP = N_HEADS // N_KV
WINDOW = 128
ROPE_THETA = 10000.0
Q_W = N_HEADS * HEAD_DIM
KV_W = N_KV * HEAD_DIM
PEER_HEADS = 8
N_KEYS = 128
N_EXPERTS = N_KEYS * N_KEYS
PEER_DK = 256
PEER_DHALF = PEER_DK // 2
PEER_TOPK = 16
EPS = 1e-6
NEG = -1e30
GATE_W = 2 * D_MODEL

LANES = 128
SUBLANES = 8
MIB = 1024 * 1024

TOKEN_TILE = 256
WIDE_TOKEN_TILE = 512
PEER_TOKEN_TILE = 512
PEER_UNIT_ROWS = 16
PEER_A_SHARE = 4
PEER_EXPERT_BLOCK = 2048
PEER_A_PER_BLOCK = PEER_EXPERT_BLOCK // N_KEYS


def _params(semantics, vmem_mib):
    return pltpu.CompilerParams(dimension_semantics=semantics, vmem_limit_bytes=vmem_mib * MIB)


def _gelu(x):
    return 0.5 * x * (1.0 + lax.erf(x * (2.0 ** -0.5)))


def _rmsnorm(x, g):
    return x * lax.rsqrt(jnp.mean(x * x, axis=-1, keepdims=True) + EPS) * g


def _mod_kernel(c_ref, w_ref, b_ref, o_ref):
    c = c_ref[...]
    s = (c * jax.nn.sigmoid(c)).astype(BF16)
    o_ref[...] = jnp.dot(s, w_ref[...].astype(BF16), preferred_element_type=F32) + b_ref[...]


def _modulation(c_all, w_mod, b_mod):
    rows = c_all.shape[0]
    nblk = w_mod.shape[1] // D_MODEL
    return pl.pallas_call(
        _mod_kernel,
        grid=(nblk,),
        in_specs=[pl.BlockSpec((rows, D_MODEL), lambda j: (0, 0)),
                  pl.BlockSpec((D_MODEL, D_MODEL), lambda j: (0, j)),
                  pl.BlockSpec((1, D_MODEL), lambda j: (0, j))],
        out_specs=pl.BlockSpec((rows, D_MODEL), lambda j: (0, j)),
        out_shape=jax.ShapeDtypeStruct((rows, w_mod.shape[1]), F32),
        compiler_params=_params(("arbitrary",), 32),
    )(c_all, w_mod, b_mod.reshape(1, -1))


def _rope(x, cos, sin_signed):
    width = x.shape[1]
    reps = width // LANES
    cosf = jnp.concatenate([cos] * reps, axis=1)
    sinf = jnp.concatenate([sin_signed] * reps, axis=1)
    half = HEAD_DIM // 2
    upper = pltpu.roll(x, width - half, axis=1)
    lower = pltpu.roll(x, half, axis=1)
    lane = lax.broadcasted_iota(jnp.int32, x.shape, 1)
    rot = jnp.where((lane & half) == 0, upper, lower)
    return x * cosf + rot * sinf


def _inproj_kernel(x_ref, sh_ref, sc_ref, g_ref, cos_ref, sin_ref, w_ref,
                   xr_ref, gr_ref, q_ref, k_ref, v_ref, gt_ref):
    x = x_ref[...]
    hn = (_rmsnorm(x, g_ref[...]) * (1.0 + sc_ref[0]) + sh_ref[0]).astype(BF16)

    def proj(c0, width):
        return jnp.dot(hn, w_ref[:, c0:c0 + width], preferred_element_type=F32)

    c = 0
    xr_ref[...] = proj(c, D_RNN); c += D_RNN
    gr_ref[...] = proj(c, D_RNN); c += D_RNN
    q_ref[...] = _rope(proj(c, Q_W), cos_ref[...], sin_ref[...]).astype(BF16); c += Q_W
    k_ref[...] = _rope(proj(c, KV_W), cos_ref[...], sin_ref[...]); c += KV_W
    v_ref[...] = proj(c, KV_W); c += KV_W
    gt_ref[...] = proj(c, GATE_W)


def _inproj(x2d, sh, sc, g, cos, sin_signed, w_in_bf16, tile, tiles_per_row, rope_tiles):
    n = x2d.shape[0]
    rmod = sh.shape[1]
    d_in = w_in_bf16.shape[1]
    tok = lambda width: pl.BlockSpec((tile, width), lambda i: (i, 0))
    mod = pl.BlockSpec((1, rmod, D_MODEL), lambda i: (i // tiles_per_row, 0, 0))
    rope = pl.BlockSpec((tile, LANES), lambda i: (i % rope_tiles, 0))
    return pl.pallas_call(
        _inproj_kernel,
        grid=(n // tile,),
        in_specs=[tok(D_MODEL), mod, mod, pl.BlockSpec((1, D_MODEL), lambda i: (0, 0)), rope, rope,
                  pl.BlockSpec((D_MODEL, d_in), lambda i: (0, 0), pipeline_mode=pl.Buffered(1))],
        out_specs=[tok(D_RNN), tok(D_RNN), tok(Q_W), tok(KV_W), tok(KV_W), tok(GATE_W)],
        out_shape=[jax.ShapeDtypeStruct((n, D_RNN), F32), jax.ShapeDtypeStruct((n, D_RNN), F32),
                   jax.ShapeDtypeStruct((n, Q_W), BF16), jax.ShapeDtypeStruct((n, KV_W), F32),
                   jax.ShapeDtypeStruct((n, KV_W), F32), jax.ShapeDtypeStruct((n, GATE_W), F32)],
        compiler_params=_params(("arbitrary",), 48),
    )(x2d, sh, sc, g, cos, sin_signed, w_in_bf16)


def _log_sigmoid(x):
    return -(jnp.maximum(-x, 0.0) + jnp.log1p(jnp.exp(-jnp.abs(x))))


def _lru_coeffs(xc, wa_ref, ba_ref, wi_ref, bi_ref, lam_ref):
    xb = xc.astype(BF16)
    r = jax.nn.sigmoid(jnp.dot(xb, wa_ref[...], preferred_element_type=F32) + ba_ref[...])
    i = jax.nn.sigmoid(jnp.dot(xb, wi_ref[...], preferred_element_type=F32) + bi_ref[...])
    log_a = LRU_C * r * _log_sigmoid(lam_ref[...])
    a = jnp.exp(log_a)
    m = 1.0 - a * a
    mult = m * lax.rsqrt(jnp.maximum(m, 1e-30))
    return a, mult * (i * xc)


def _shift_rows_in_blocks(x, s, fill):
    rolled = pltpu.roll(x, s, axis=1)
    row = lax.broadcasted_iota(jnp.int32, x.shape, 1)
    return jnp.where(row < s, fill, rolled)


def _rnn_prompt_kernel(xr_ref, gr_ref, cw_ref, cb_ref, wa_ref, ba_ref, wi_ref, bi_ref, lam_ref,
                       hg_ref, hlast_ref, tail_ref, hc_ref):
    t = pl.program_id(1)
    tile = xr_ref.shape[0]

    @pl.when(t == 0)
    def _():
        tail_ref[...] = jnp.zeros_like(tail_ref)
        hc_ref[...] = jnp.zeros_like(hc_ref)

    xr = xr_ref[...]
    cat = jnp.concatenate([tail_ref[...], xr], axis=0)
    xc = cb_ref[...] + cw_ref[CONV_W - 1:CONV_W, :] * xr
    for s in range(1, CONV_W):
        shifted = pltpu.roll(cat, s, axis=0)[SUBLANES:, :]
        xc = xc + cw_ref[CONV_W - 1 - s:CONV_W - s, :] * shifted
    tail_ref[...] = xr[tile - SUBLANES:, :]

    a, b = _lru_coeffs(xc, wa_ref, ba_ref, wi_ref, bi_ref, lam_ref)
    nblocks = tile // SUBLANES
    a = a.reshape(nblocks, SUBLANES, D_RNN)
    b = b.reshape(nblocks, SUBLANES, D_RNN)
    s = 1
    while s < SUBLANES:
        a_prev = _shift_rows_in_blocks(a, s, 1.0)
        b_prev = _shift_rows_in_blocks(b, s, 0.0)
        b = a * b_prev + b
        a = a * a_prev
        s *= 2
    carry = hc_ref[0:1, :]
    blocks = []
    for k in range(nblocks):
        blocks.append(a[k] * carry + b[k])
        carry = a[k, SUBLANES - 1:, :] * carry + b[k, SUBLANES - 1:, :]
    h = jnp.concatenate(blocks, axis=0)
    hc_ref[...] = jnp.broadcast_to(carry, hc_ref.shape)
    hlast_ref[0] = carry
    hg_ref[...] = (h * _gelu(gr_ref[...])).astype(BF16)


def _rnn_prompt(xr, gr, conv_w, conv_b, wa, ba, wi, bi, lam, nb, seq, tile):
    n = xr.shape[0]
    tpr = seq // tile
    tok = pl.BlockSpec((tile, D_RNN), lambda b, t: (b * tpr + t, 0))
    row = pl.BlockSpec((1, D_RNN), lambda b, t: (0, 0))
    sq = pl.BlockSpec((D_RNN, D_RNN), lambda b, t: (0, 0))
    return pl.pallas_call(
        _rnn_prompt_kernel,
        grid=(nb, tpr),
        in_specs=[tok, tok, pl.BlockSpec((CONV_W, D_RNN), lambda b, t: (0, 0)), row, sq, row, sq, row, row],
        out_specs=[tok, pl.BlockSpec((1, 1, D_RNN), lambda b, t: (b, 0, 0))],
        out_shape=[jax.ShapeDtypeStruct((n, D_RNN), BF16), jax.ShapeDtypeStruct((nb, 1, D_RNN), F32)],
        scratch_shapes=[pltpu.VMEM((SUBLANES, D_RNN), F32), pltpu.VMEM((SUBLANES, D_RNN), F32)],
        compiler_params=_params(("arbitrary", "arbitrary"), 48),
    )(xr, gr, conv_w, conv_b, wa, ba, wi, bi, lam)


def _rnn_sample_kernel(xr_ref, gr_ref, c0_ref, c1_ref, c2_ref, h0_ref, cw_ref, cb_ref,
                       wa_ref, ba_ref, wi_ref, bi_ref, lam_ref, hg_ref, h_ref):
    xc = (cb_ref[...] + cw_ref[0:1, :] * c0_ref[...] + cw_ref[1:2, :] * c1_ref[...]
          + cw_ref[2:3, :] * c2_ref[...] + cw_ref[3:4, :] * xr_ref[...])
    a, b = _lru_coeffs(xc, wa_ref, ba_ref, wi_ref, bi_ref, lam_ref)
    h = b + a * h0_ref[...]
    h_ref[...] = h
    hg_ref[...] = (h * _gelu(gr_ref[...])).astype(BF16)


def _rnn_sample(xr, gr, c0, c1, c2, h0, conv_w, conv_b, wa, ba, wi, bi, lam):
    n = xr.shape[0]
    return pl.pallas_call(
        _rnn_sample_kernel,
        out_shape=[jax.ShapeDtypeStruct((n, D_RNN), BF16), jax.ShapeDtypeStruct((n, D_RNN), F32)],
        compiler_params=pltpu.CompilerParams(vmem_limit_bytes=48 * MIB),
    )(xr, gr, c0, c1, c2, h0, conv_w, conv_b, wa, ba, wi, bi, lam)


ATTN_BLOCKS_PER_STEP = 2


def _attn_prompt_kernel(sink_ref, q_ref, kp_ref, kc_ref, vp_ref, vc_ref, o_ref):
    step = pl.program_id(1)
    kk = jnp.concatenate([kp_ref[...], kc_ref[...]], axis=0).astype(BF16)
    vv = jnp.concatenate([vp_ref[...], vc_ref[...]], axis=0).astype(BF16)
    qi = lax.broadcasted_iota(jnp.int32, (WINDOW, 2 * WINDOW), 0)
    ci = lax.broadcasted_iota(jnp.int32, (WINDOW, 2 * WINDOW), 1)
    dist = qi + WINDOW - ci
    band = (dist >= 0) & (dist <= WINDOW)
    for u in range(ATTN_BLOCKS_PER_STEP):
        j = step * ATTN_BLOCKS_PER_STEP + u
        mask = band & (ci + (j - 1) * WINDOW >= 0)
        q = q_ref[pl.ds(u * WINDOW, WINDOW), :]
        outs = []
        for h in range(N_HEADS):
            g = h // GROUP
            qh = q[:, h * HEAD_DIM:(h + 1) * HEAD_DIM]
            kh = kk[u * WINDOW:(u + 2) * WINDOW, g * HEAD_DIM:(g + 1) * HEAD_DIM]
            vh = vv[u * WINDOW:(u + 2) * WINDOW, g * HEAD_DIM:(g + 1) * HEAD_DIM]
            s = lax.dot_general(qh, kh, (((1,), (1,)), ((), ())), preferred_element_type=F32) * (HEAD_DIM ** -0.5)
            s = jnp.where(mask, s, NEG)
            sink = sink_ref[h]
            m = jnp.maximum(jnp.max(s, axis=-1, keepdims=True), sink)
            e = jnp.exp(s - m)
            p = e / (jnp.sum(e, axis=-1, keepdims=True) + jnp.exp(sink - m))
            outs.append(jnp.dot(p.astype(BF16), vh, preferred_element_type=F32))
        o_ref[pl.ds(u * WINDOW, WINDOW), :] = jnp.concatenate(outs, axis=1).astype(BF16)


def _attn_prompt(sinks, q, k, v, nb, seq):
    n = q.shape[0]
    per = ATTN_BLOCKS_PER_STEP
    nstep = seq // (per * WINDOW)
    cur = lambda b, j: (b * nstep + j, 0)
    prev = lambda b, j: (jnp.maximum((b * nstep + j) * per - 1, b * nstep * per), 0)
    return pl.pallas_call(
        _attn_prompt_kernel,
        grid=(nb, nstep),
        in_specs=[pl.BlockSpec(memory_space=pltpu.SMEM),
                  pl.BlockSpec((per * WINDOW, Q_W), cur),
                  pl.BlockSpec((WINDOW, KV_W), prev), pl.BlockSpec((per * WINDOW, KV_W), cur),
                  pl.BlockSpec((WINDOW, KV_W), prev), pl.BlockSpec((per * WINDOW, KV_W), cur)],
        out_specs=pl.BlockSpec((per * WINDOW, Q_W), cur),
        out_shape=jax.ShapeDtypeStruct((n, Q_W), BF16),
        compiler_params=_params(("arbitrary", "arbitrary"), 32),
    )(sinks, q, k, k, v, v)


SAMPLE_ATTN_ROWS = 16


def _attn_sample_kernel(sink_ref, qx_ref, ck_ref, cv_ref, kn_ref, vn_ref, o_ref, wk_ref, wv_ref):
    rows = SAMPLE_ATTN_ROWS
    wbuf = ck_ref.shape[1]
    last = lax.broadcasted_iota(jnp.int32, (wbuf, KV_W), 0) == wbuf - 1
    scale = HEAD_DIM ** -0.5
    for r in range(rows):
        wk_ref[r] = jnp.where(last, kn_ref[r], pltpu.roll(ck_ref[r], wbuf - 1, axis=0))
        wv_ref[r] = jnp.where(last, vn_ref[r], pltpu.roll(cv_ref[r], wbuf - 1, axis=0))
    s = jnp.concatenate(
        [lax.dot_general(qx_ref[r], ck_ref[r].astype(BF16), (((1,), (1,)), ((), ())), preferred_element_type=F32)
         for r in range(rows)], axis=0) * scale
    qx = jnp.concatenate([qx_ref[r] for r in range(rows)], axis=0).astype(F32)
    kn = jnp.concatenate([jnp.broadcast_to(kn_ref[r].astype(BF16).astype(F32), (N_HEADS, KV_W))
                          for r in range(rows)], axis=0)
    sink = jnp.concatenate([sink_ref[...]] * rows, axis=0)
    s_new = jnp.sum(qx * kn, axis=-1, keepdims=True) * scale
    m = jnp.maximum(jnp.maximum(jnp.max(s, axis=-1, keepdims=True), s_new), sink)
    e = jnp.exp(s - m)
    e_new = jnp.exp(s_new - m)
    den = jnp.sum(e, axis=-1, keepdims=True) + e_new + jnp.exp(sink - m)
    p = (e / den).astype(BF16)
    p_new = (e_new / den).astype(BF16).astype(F32)
    for r in range(rows):
        blk = slice(r * N_HEADS, (r + 1) * N_HEADS)
        vn = vn_ref[r].astype(BF16).astype(F32)
        o_ref[r] = jnp.dot(p[blk], cv_ref[r].astype(BF16), preferred_element_type=F32) + p_new[blk] * vn


def _attn_sample(sinks_col, qx, cache_k, cache_v, k_new, v_new):
    nb, wbuf = cache_k.shape[0], cache_k.shape[1]
    rows = SAMPLE_ATTN_ROWS
    blk = lambda d1, d2: pl.BlockSpec((rows, d1, d2), lambda i: (i, 0, 0))
    return pl.pallas_call(
        _attn_sample_kernel,
        grid=(nb // rows,),
        in_specs=[pl.BlockSpec((N_HEADS, 1), lambda i: (0, 0)), blk(N_HEADS, KV_W),
                  blk(wbuf, KV_W), blk(wbuf, KV_W), blk(1, KV_W), blk(1, KV_W)],
        out_specs=[blk(N_HEADS, KV_W), blk(wbuf, KV_W), blk(wbuf, KV_W)],
        out_shape=[jax.ShapeDtypeStruct((nb, N_HEADS, KV_W), F32),
                   jax.ShapeDtypeStruct((nb, wbuf, KV_W), F32), jax.ShapeDtypeStruct((nb, wbuf, KV_W), F32)],
        compiler_params=_params(("arbitrary",), 32),
    )(sinks_col, qx, cache_k, cache_v, k_new, v_new)


def _merge_kernel(hg_ref, o_ref, gt_ref, x_ref, g1_ref, sh_ref, sc_ref, gn_ref, wr_ref, wa_ref, wo_ref,
                  x1_ref, h2t_ref):
    y_r = jnp.dot(hg_ref[...], wr_ref[...], preferred_element_type=F32)
    y_a = jnp.dot(o_ref[...], wa_ref[...], preferred_element_type=F32)
    merged = (jax.nn.sigmoid(gt_ref[:, :D_MODEL]) * y_r + jax.nn.sigmoid(gt_ref[:, D_MODEL:]) * y_a)
    x1 = x_ref[...] + g1_ref[0] * jnp.dot(merged.astype(BF16), wo_ref[...], preferred_element_type=F32)
    x1_ref[...] = x1
    h2 = _rmsnorm(x1, gn_ref[...]) * (1.0 + sc_ref[0]) + sh_ref[0]
    h2t_ref[...] = h2.T.astype(BF16)


def _merge(hg, o, gates, x2d, gt1, sh2, sc2, g2, wr, wa, wo, tile, tiles_per_row):
    n = x2d.shape[0]
    rmod = gt1.shape[1]
    tok = lambda width: pl.BlockSpec((tile, width), lambda i: (i, 0))
    mod = pl.BlockSpec((1, rmod, D_MODEL), lambda i: (i // tiles_per_row, 0, 0))
    full = lambda a: pl.BlockSpec(a.shape, lambda i: (0, 0))
    return pl.pallas_call(
        _merge_kernel,
        grid=(n // tile,),
        in_specs=[tok(D_RNN), tok(Q_W), tok(GATE_W), tok(D_MODEL), mod, mod, mod,
                  pl.BlockSpec((1, D_MODEL), lambda i: (0, 0)), full(wr), full(wa), full(wo)],
        out_specs=[tok(D_MODEL), pl.BlockSpec((D_MODEL, tile), lambda i: (0, i))],
        out_shape=[jax.ShapeDtypeStruct((n, D_MODEL), F32), jax.ShapeDtypeStruct((D_MODEL, n), BF16)],
        compiler_params=_params(("arbitrary",), 48),
    )(hg, o, gates, x2d, gt1, sh2, sc2, g2, wr, wa, wo)


def _vmax(a, b):
    if a is None:
        return b
    if b is None:
        return a
    return jnp.maximum(a, b)


def _vmin(a, b):
    if a is None or b is None:
        return None
    return jnp.minimum(a, b)


def _compare_exchange(xs, i, j):
    hi, lo = _vmax(xs[i], xs[j]), _vmin(xs[i], xs[j])
    xs[i], xs[j] = hi, lo


def _sort_pairs(n):
    pairs = []

    def merge(lo, hi, r):
        step = r * 2
        if step < hi - lo:
            merge(lo, hi, step)
            merge(lo + r, hi, step)
            for i in range(lo + r, hi - r, step):
                pairs.append((i, i + r))
        else:
            pairs.append((lo, lo + r))

    def sort(lo, hi):
        if hi - lo >= 1:
            mid = lo + (hi - lo) // 2
            sort(lo, mid)
            sort(mid + 1, hi)
            merge(lo, hi, 1)

    sort(0, n - 1)
    return pairs


_SORT16 = _sort_pairs(PEER_TOPK)


def _sort_desc(xs):
    xs = list(xs)
    for i, j in _SORT16:
        _compare_exchange(xs, i, j)
    return xs


def _bitonic_to_desc(xs):
    xs = list(xs)
    d = len(xs) // 2
    while d >= 1:
        for i in range(len(xs)):
            if (i & d) == 0:
                _compare_exchange(xs, i, i + d)
        d //= 2
    return xs


def _top_merge(xs, ys):
    k = len(xs)
    return _bitonic_to_desc([_vmax(xs[i], ys[k - 1 - i]) for i in range(k)])


def _top16_rows(s):
    rows = _sort_desc([s[r * SUBLANES:(r + 1) * SUBLANES, :] for r in range(N_KEYS // SUBLANES)])
    shift = SUBLANES // 2
    while shift >= 1:
        rows = _top_merge(rows, [pltpu.roll(x, shift, axis=0) for x in rows])
        shift //= 2
    return rows


def _top16_pair_sums(v1, v2):
    k = PEER_TOPK
    col = lambda j: [v1[i] + v2[j] for i in range(k // (j + 1))]
    g0 = col(0)
    row0_tail = [v1[0] + v2[j] for j in range(k // 2, k)]
    g1 = _bitonic_to_desc(col(1) + row0_tail[::-1])
    g2 = _sort_desc(col(2) + col(3) + col(4) + col(5) + col(6))
    g3 = col(7) + [None] * (k - 2)
    return _top_merge(_top_merge(g0, g1), _top_merge(g2, g3))


PEER_SPECIAL_RANKS = 4


def _threshold_chain(s1_tile, v2_head, tau):
    thr = jnp.full(s1_tile.shape, jnp.inf, F32)
    for v in v2_head:
        thr = jnp.where(s1_tile + v >= tau, v, thr)
    return thr


def _route_kernel(h2t_ref, wq_ref, keys_ref, thr_ref, c1_ref, s2_ref, e2_ref, qt_ref):
    qt_ref[...] = jnp.dot(wq_ref[...], h2t_ref[...], preferred_element_type=F32)
    k = PEER_TOPK
    groups = h2t_ref.shape[1] // LANES
    heads_per_iter = SUBLANES // groups
    sub = lax.broadcasted_iota(jnp.int32, (SUBLANES, LANES), 0)
    generic = k // (PEER_SPECIAL_RANKS + 1)

    def head_group(it, carry):
        packed1, packed2, kept = [None] * k, [None] * k, []
        for hh in range(heads_per_iter):
            h = it * heads_per_iter + hh
            base = pl.multiple_of(h * PEER_DK, PEER_DK)
            q1 = qt_ref[pl.ds(base, PEER_DHALF), :].astype(BF16)
            q2 = qt_ref[pl.ds(base + PEER_DHALF, PEER_DHALF), :].astype(BF16)
            s1 = jnp.dot(keys_ref[h, 0], q1, preferred_element_type=F32)
            s2 = jnp.dot(keys_ref[h, 1], q2, preferred_element_type=F32)
            c1_ref[h] = s1
            s2_ref[h] = s2
            for g in range(groups):
                v1 = _top16_rows(s1[:, g * LANES:(g + 1) * LANES])
                v2 = _top16_rows(s2[:, g * LANES:(g + 1) * LANES])
                slot = hh * groups + g
                for i in range(k):
                    packed1[i] = v1[i] if slot == 0 else jnp.where(sub == slot, v1[i], packed1[i])
                    packed2[i] = v2[i] if slot == 0 else jnp.where(sub == slot, v2[i], packed2[i])
                kept.append((h, g, v1[:PEER_SPECIAL_RANKS], v2[:generic]))

        top = _top16_pair_sums(packed1, packed2)
        tau = top[k - 1]
        z = jnp.ones_like(top[0])
        for f in top[1:]:
            z = z + jnp.exp(f - top[0])
        half_rz = 0.5 / z
        by_rank = [_threshold_chain(packed1[i], packed2[:k // (i + 1)], tau) for i in range(PEER_SPECIAL_RANKS)]

        for slot, (h, g, v1_top, v2_top) in enumerate(kept):
            unpack = lambda x: jnp.broadcast_to(x[slot:slot + 1, :], (SUBLANES, LANES))
            tau_s, rank_s, rz_s = unpack(tau), [unpack(t) for t in by_rank], unpack(half_rz)
            lanes = pl.ds(g * LANES, LANES)
            for r in range(N_KEYS // SUBLANES):
                rows = pl.ds(r * SUBLANES, SUBLANES)
                s1_tile = c1_ref[h, rows, lanes]
                thr = _threshold_chain(s1_tile, v2_top, tau_s)
                for i in reversed(range(PEER_SPECIAL_RANKS)):
                    thr = jnp.where(s1_tile >= v1_top[i], rank_s[i], thr)
                thr_ref[h, rows, lanes] = thr
                c1_ref[h, rows, lanes] = jnp.exp(s1_tile - v1_top[0]) * rz_s
            e2_ref[h, :, lanes] = jnp.exp(s2_ref[h, :, lanes] - v2_top[0][0:1, :])
        return carry

    lax.fori_loop(0, PEER_HEADS // heads_per_iter, head_group, 0)


def _route(h2t, wq_t, keys_bf16, tile):
    n = h2t.shape[1]
    tab = pl.BlockSpec((PEER_HEADS, N_KEYS, tile), lambda i: (0, 0, i))
    tab_shape = jax.ShapeDtypeStruct((PEER_HEADS, N_KEYS, n), F32)
    return pl.pallas_call(
        _route_kernel,
        grid=(n // tile,),
        in_specs=[pl.BlockSpec((D_MODEL, tile), lambda i: (0, i)),
                  pl.BlockSpec(wq_t.shape, lambda i: (0, 0)),
                  pl.BlockSpec(keys_bf16.shape, lambda i: (0, 0, 0, 0))],
        out_specs=[tab, tab, tab, tab],
        out_shape=[tab_shape, tab_shape, tab_shape, tab_shape],
        scratch_shapes=[pltpu.VMEM((PEER_HEADS * PEER_DK, tile), F32)],
        compiler_params=_params(("arbitrary",), 48),
    )(h2t, wq_t, keys_bf16)


def _peer_kernel(h2t_ref, u_ref, vt_ref, thr_ref, c1_ref, s2_ref, e2_ref, x1_ref, g2_ref, gf_ref,
                 y_ref, out_ref, act_ref, wgt_ref):
    j = pl.program_id(1)
    tile = h2t_ref.shape[1]

    @pl.when(j == 0)
    def _():
        out_ref[...] = jnp.zeros_like(out_ref)

    half = PEER_EXPERT_BLOCK // 2
    for r0 in (0, half):
        act_ref[pl.ds(r0, half), :] = jnp.dot(u_ref[pl.ds(r0, half), :], h2t_ref[...], preferred_element_type=F32)

    def row_group(r, carry):
        r0 = pl.multiple_of(r * PEER_UNIT_ROWS, PEER_UNIT_ROWS)
        rows = pl.ds(r0, PEER_UNIT_ROWS)
        for a0 in range(0, PEER_A_PER_BLOCK, PEER_A_SHARE):
            gates = [[jnp.zeros((PEER_UNIT_ROWS, LANES), F32) for _ in range(tile // LANES)] for _ in range(PEER_A_SHARE)]
            for h in range(PEER_HEADS):
                for g in range(tile // LANES):
                    sl = pl.ds(g * LANES, LANES)
                    s2 = s2_ref[h, rows, sl]
                    e2 = e2_ref[h, rows, sl]
                    for i in range(PEER_A_SHARE):
                        a = a0 + i
                        sel = jnp.where(s2 >= thr_ref[h, a:a + 1, sl], e2, 0.0)
                        gates[i][g] = gates[i][g] + sel * c1_ref[h, a:a + 1, sl]
            for i in range(PEER_A_SHARE):
                erows = pl.ds((a0 + i) * N_KEYS + r0, PEER_UNIT_ROWS)
                for g in range(tile // LANES):
                    sl = pl.ds(g * LANES, LANES)
                    x = act_ref[erows, sl]
                    wgt_ref[erows, sl] = (gates[i][g] * (x * (1.0 + lax.erf(x * (2.0 ** -0.5))))).astype(BF16)
        return carry

    lax.fori_loop(0, N_KEYS // PEER_UNIT_ROWS, row_group, 0)
    for d0 in (0, D_MODEL // 2):
        out_ref[pl.ds(d0, D_MODEL // 2), :] += jnp.dot(vt_ref[pl.ds(d0, D_MODEL // 2), :], wgt_ref[...],
                                                       preferred_element_type=F32)

    @pl.when(j == pl.num_programs(1) - 1)
    def _():
        x2 = x1_ref[...] + g2_ref[0] * out_ref[...].T
        y_ref[...] = _rmsnorm(x2, gf_ref[...])


def _peer(h2t, u_bf16, vt_bf16, thr, c1, s2, e2, x1, gt2, g_final, tile, tiles_per_row):
    n = h2t.shape[1]
    eb = PEER_EXPERT_BLOCK
    rmod = gt2.shape[1]
    tab = lambda rows: pl.BlockSpec((PEER_HEADS, rows, tile), lambda i, j: (0, 0, i))
    ablk = pl.BlockSpec((PEER_HEADS, PEER_A_PER_BLOCK, tile), lambda i, j: (0, j, i))
    tok = pl.BlockSpec((tile, D_MODEL), lambda i, j: (i, 0))
    return pl.pallas_call(
        _peer_kernel,
        grid=(n // tile, N_EXPERTS // eb),
        in_specs=[pl.BlockSpec((D_MODEL, tile), lambda i, j: (0, i)),
                  pl.BlockSpec((eb, D_MODEL), lambda i, j: (j, 0)),
                  pl.BlockSpec((D_MODEL, eb), lambda i, j: (0, j)),
                  ablk, ablk, tab(N_KEYS), tab(N_KEYS), tok,
                  pl.BlockSpec((1, rmod, D_MODEL), lambda i, j: (i // tiles_per_row, 0, 0)),
                  pl.BlockSpec((1, D_MODEL), lambda i, j: (0, 0))],
        out_specs=tok,
        out_shape=jax.ShapeDtypeStruct((n, D_MODEL), F32),
        scratch_shapes=[pltpu.VMEM((D_MODEL, tile), F32), pltpu.VMEM((eb, tile), F32),
                        pltpu.VMEM((eb, tile), BF16)],
        compiler_params=_params(("arbitrary", "arbitrary"), 56),
    )(h2t, u_bf16, vt_bf16, thr, c1, s2, e2, x1, gt2, g_final)


def _rope_tables(pos):
    half = HEAD_DIM // 2
    inv = ROPE_THETA ** (-jnp.arange(half, dtype=F32) / half)
    ang = pos.astype(F32)[:, None] * inv[None, :]
    cos, sin = jnp.cos(ang), jnp.sin(ang)
    reps = LANES // HEAD_DIM
    return (jnp.tile(jnp.concatenate([cos, cos], axis=1), (1, reps)),
            jnp.tile(jnp.concatenate([-sin, sin], axis=1), (1, reps)))


def _block_diag(w):
    eye = jnp.eye(RNN_BLOCKS, dtype=w.dtype)
    return (eye[:, None, :, None] * w[:, :, None, :]).reshape(D_RNN, D_RNN)


def _channel_mixer(h2t, x1, gt2, g_final, wq_t, keys, u_bf16, vt_bf16, tile, peer_tile, peer_tiles_per_row):
    thr, c1, s2, e2 = _route(h2t, wq_t, keys, tile)
    return _peer(h2t, u_bf16, vt_bf16, thr, c1, s2, e2, x1, gt2, g_final, peer_tile, peer_tiles_per_row)


def kernel(x_prompt, x_sample, state_conv, state_rnn, cache_win_k, cache_win_v, c_prompt, c_sample, g_norm1, g_norm2, g_final, w_mod, b_mod, w_in, conv_w, conv_b, w_a, b_a, w_i, b_i, lam, w_rnn_out, sinks, w_attn_out, w_out, w_pq, sub_keys, u_tab, v_tab):
    nb, seq, _ = x_prompt.shape
    ns = x_sample.shape[0]
    wbuf = cache_win_k.shape[2]
    assert x_sample.shape[1] == 1 and g_norm1.shape[0] == 1
    assert seq % TOKEN_TILE == 0 and seq % WIDE_TOKEN_TILE == 0 and seq % (ATTN_BLOCKS_PER_STEP * WINDOW) == 0 and seq % WINDOW == 0 and ns % LANES == 0 and seq % PEER_TOKEN_TILE == 0
    l = 0

    w_in_b = w_in[l].astype(BF16)
    wa_d = _block_diag(w_a[l]).astype(BF16)
    wi_d = _block_diag(w_i[l]).astype(BF16)
    wr_b, wat_b, wo_b = w_rnn_out[l].astype(BF16), w_attn_out[l].astype(BF16), w_out[l].astype(BF16)
    wq_t = w_pq[l].T.astype(BF16)
    keys_b = sub_keys[l].astype(BF16)
    u_b = u_tab[l].astype(BF16)
    vt_b = v_tab[l].T.astype(BF16)
    row = lambda v: v.reshape(1, -1)
    g1, g2, gf = row(g_norm1[l]), row(g_norm2[l]), row(g_final)
    cb, ba, bi, lm = row(conv_b[l]), row(b_a[l]), row(b_i[l]), row(lam[l])

    mod = _modulation(jnp.concatenate([c_prompt, c_sample], axis=0), w_mod[l], b_mod[l])
    mod_p = [m.reshape(nb, 1, D_MODEL) for m in jnp.split(mod[:nb], 6, axis=-1)]
    mod_s = [m.reshape(1, ns, D_MODEL) for m in jnp.split(mod[nb:], 6, axis=-1)]

    tile = TOKEN_TILE
    tpr = seq // tile
    xp = x_prompt.reshape(nb * seq, D_MODEL)
    cos_p, sin_p = _rope_tables(jnp.arange(seq))
    wide = WIDE_TOKEN_TILE
    xr, gr, q, k, v, gates = _inproj(xp, mod_p[0], mod_p[1], g1, cos_p, sin_p, w_in_b, wide, seq // wide, seq // wide)
    hg, rnn_p = _rnn_prompt(xr, gr, conv_w[l], cb, wa_d, ba, wi_d, bi, lm, nb, seq, tile)
    o = _attn_prompt(sinks[l], q, k, v, nb, seq)
    x1, h2t = _merge(hg, o, gates, xp, mod_p[2], mod_p[3], mod_p[4], g2, wr_b, wat_b, wo_b, wide, seq // wide)
    y_p = _channel_mixer(h2t, x1, mod_p[5], gf, wq_t, keys_b, u_b, vt_b, wide, PEER_TOKEN_TILE,
                         seq // PEER_TOKEN_TILE)
    y_prompt = y_p.reshape(nb, seq, D_MODEL)
    conv_prompt = xr.reshape(nb, seq, D_RNN)[:, seq - (CONV_W - 1):][None]
    rnn_prompt = rnn_p.reshape(1, nb, D_RNN)
    nbuf = min(WINDOW, seq)
    win_k_prompt = k.reshape(nb, seq, N_KV, HEAD_DIM)[:, seq - nbuf:][None]
    win_v_prompt = v.reshape(nb, seq, N_KV, HEAD_DIM)[:, seq - nbuf:][None]

    xs = x_sample.reshape(ns, D_MODEL)
    cos_s, sin_s = _rope_tables(jnp.full((ns,), PAST_LEN))
    xr_s, gr_s, q_s, k_s, v_s, gates_s = _inproj(xs, mod_s[0], mod_s[1], g1, cos_s, sin_s, w_in_b, ns, 1, 1)
    sc = state_conv[l]
    hg_s, h_s = _rnn_sample(xr_s, gr_s, sc[:, 0], sc[:, 1], sc[:, 2], state_rnn[l],
                            conv_w[l], cb, wa_d, ba, wi_d, bi, lm)
    head_group = (jnp.arange(N_HEADS)[:, None] // GROUP == jnp.arange(N_KV)[None, :])
    qx = jnp.where(head_group[None, :, :, None], q_s.reshape(ns, N_HEADS, 1, HEAD_DIM), 0)
    qx = qx.reshape(ns, N_HEADS, KV_W).astype(BF16)
    ck = cache_win_k[l].reshape(ns, wbuf, KV_W)
    cv = cache_win_v[l].reshape(ns, wbuf, KV_W)
    ox, wk_s, wv_s = _attn_sample(sinks[l].reshape(N_HEADS, 1), qx, ck, cv,
                                  k_s.reshape(ns, 1, KV_W), v_s.reshape(ns, 1, KV_W))
    o_s = jnp.where(head_group[None, :, :, None], ox.reshape(ns, N_HEADS, N_KV, HEAD_DIM), 0.0).sum(axis=2)
    o_s = o_s.reshape(ns, Q_W).astype(BF16)
    x1_s, h2t_s = _merge(hg_s, o_s, gates_s, xs, mod_s[2], mod_s[3], mod_s[4], g2, wr_b, wat_b, wo_b, ns, 1)
    y_s = _channel_mixer(h2t_s, x1_s, mod_s[5], gf, wq_t, keys_b, u_b, vt_b, ns, ns, 1)
    y_sample = y_s.reshape(ns, 1, D_MODEL)
    conv_sample = jnp.concatenate([sc[:, 1:], xr_s[:, None, :]], axis=1)[None]
    rnn_sample = h_s[None]
    win_k_sample = wk_s.reshape(1, ns, wbuf, N_KV, HEAD_DIM)
    win_v_sample = wv_s.reshape(1, ns, wbuf, N_KV, HEAD_DIM)

    return (y_prompt, y_sample, conv_prompt, rnn_prompt, win_k_prompt, win_v_prompt,
            conv_sample, rnn_sample, win_k_sample, win_v_sample)
```
